```python
import functools
import jax, jax.numpy as jnp
from jax import lax
import numpy as np

D_MODEL = 1024
BATCH = 4
SEQ = 4096
DEPTH = 4
DEC_BATCH = 32
DEC_SEQ = 16
PAST_LEN = 1024

CHUNK = 64
N_META = 16
D_MIX = D_MODEL
D_A = D_MIX // 4
D_B = D_MIX // 2
D_C = D_MIX // 4
HEAD_DIM = 64
N_HEADS = D_B // HEAD_DIM
W_A = 3
W_C = 31
W_F = 3
D_FF = ((8 * D_MODEL // 3 + 255) // 256) * 256
D_IN = 3 * D_A + 3 * D_B + N_HEADS + 2 * D_C
Q_BLOCK = 128
FORGET_BIAS = 3.0
EPS = 1e-6

kernel_name = "hymba_conv_fox_conformer_stream_step"


def rmsnorm(x, g):
    xf = x.astype(jnp.float32)
    y = xf * lax.rsqrt(jnp.mean(xf * xf, axis=-1, keepdims=True) + EPS)
    return (y * g.astype(jnp.float32)).astype(x.dtype)


def layernorm(x, g, b):
    xf = x.astype(jnp.float32)
    mu = jnp.mean(xf, axis=-1, keepdims=True)
    xc = xf - mu
    y = xc * lax.rsqrt(jnp.mean(xc * xc, axis=-1, keepdims=True) + EPS)
    return (y * g.astype(jnp.float32) + b.astype(jnp.float32)).astype(x.dtype)


def causal_dwconv(x_full, w):
    channels = x_full.shape[-1]
    return lax.conv_general_dilated(
        x_full, w[:, None, :].astype(x_full.dtype), window_strides=(1,), padding='VALID',
        dimension_numbers=('NWC', 'WIO', 'NWC'), feature_group_count=channels)


def fox_attend(q, k, v, c_q, c_k, t_q, s_k):
    logits = jnp.einsum('bqhd,bshd->bhqs', q, k).astype(jnp.float32) * (HEAD_DIM ** -0.5)
    bias = jnp.transpose(c_q, (0, 2, 1))[..., :, None] - jnp.transpose(c_k, (0, 2, 1))[..., None, :]
    mask = s_k[None, :] <= t_q[:, None]
    logits = jnp.where(mask, logits + bias, -jnp.inf)
    p = jax.nn.softmax(logits, axis=-1)
    return jnp.einsum('bhqs,bshd->bqhd', p.astype(v.dtype), v)


def fox_prompt(q, k, v, logf):
    bsz, L, H, D = q.shape
    nb = -(-L // Q_BLOCK)
    pad = nb * Q_BLOCK - L
    c = jnp.cumsum(logf, axis=1)
    q_p = jnp.pad(q, ((0, 0), (0, pad), (0, 0), (0, 0)))
    c_p = jnp.pad(c, ((0, 0), (0, pad), (0, 0)), mode='edge')
    qb = jnp.transpose(q_p.reshape(bsz, nb, Q_BLOCK, H, D), (1, 0, 2, 3, 4))
    cb = jnp.transpose(c_p.reshape(bsz, nb, Q_BLOCK, H), (1, 0, 2, 3))
    tb = jnp.arange(nb * Q_BLOCK).reshape(nb, Q_BLOCK)
    s_k = jnp.arange(L)

    def one_block(args):
        q_blk, c_blk, t_blk = args
        return fox_attend(q_blk, k, v, c_blk, c, t_blk, s_k)

    o = lax.map(one_block, (qb, cb, tb))
    return jnp.transpose(o, (1, 0, 2, 3, 4)).reshape(bsz, nb * Q_BLOCK, H, D)[:, :L]


def fox_sample(ck, cv, clogf, q, k, v, logf):
    P = ck.shape[1]
    S = q.shape[1]
    k_all = jnp.concatenate([ck.astype(k.dtype), k], axis=1)
    v_all = jnp.concatenate([cv.astype(v.dtype), v], axis=1)
    lf_all = jnp.concatenate([clogf.astype(jnp.float32), logf], axis=1)
    c = jnp.cumsum(lf_all, axis=1)
    return fox_attend(q, k_all, v_all, c[:, P:], c, P + jnp.arange(S), jnp.arange(P + S))


def layer(x, buf_a, buf_c, buf_f, attend, norm1_g, w_in, b_f, q_norm_g, k_norm_g, w_conv_a,
          w_conv_c, b_conv_c, ln_c_g, ln_c_b, out_g_a, out_g_b, out_g_c, w_out, norm2_g,
          w_up, w_conv_f, w_down):
    bsz, T, _ = x.shape
    h = rmsnorm(x, norm1_g)
    z = h @ w_in.astype(h.dtype)
    o1 = D_A; o2 = 2 * D_A; o3 = 3 * D_A
    o4 = o3 + D_B; o5 = o4 + D_B; o6 = o5 + D_B; o7 = o6 + N_HEADS; o8 = o7 + D_C
    gb, gc, hv, q, k, v, fl, pa, pb = jnp.split(z, [o1, o2, o3, o4, o5, o6, o7, o8], axis=-1)

    u = gc * hv
    u_full = jnp.concatenate([buf_a.astype(u.dtype), u], axis=1)
    a_out = gb * causal_dwconv(u_full, w_conv_a)

    q = rmsnorm(q.reshape(bsz, T, N_HEADS, HEAD_DIM), q_norm_g)
    k = rmsnorm(k.reshape(bsz, T, N_HEADS, HEAD_DIM), k_norm_g)
    v = v.reshape(bsz, T, N_HEADS, HEAD_DIM)
    logf = jax.nn.log_sigmoid(fl.astype(jnp.float32) + b_f.astype(jnp.float32))
    b_out = attend(q, k, v, logf).reshape(bsz, T, D_B)

    g = pa * jax.nn.sigmoid(pb)
    g_full = jnp.concatenate([buf_c.astype(g.dtype), g], axis=1)
    cc = causal_dwconv(g_full, w_conv_c) + b_conv_c.astype(g.dtype)
    c_out = jax.nn.silu(layernorm(cc, ln_c_g, ln_c_b))

    merged = jnp.concatenate([rmsnorm(a_out, out_g_a), rmsnorm(b_out, out_g_b),
                              rmsnorm(c_out, out_g_c)], axis=-1)
    x = x + merged @ w_out.astype(merged.dtype)

    h2 = rmsnorm(x, norm2_g)
    up = h2 @ w_up.astype(h2.dtype)
    up_full = jnp.concatenate([buf_f.astype(up.dtype), up], axis=1)
    gate, val = jnp.split(causal_dwconv(up_full, w_conv_f), 2, axis=-1)
    x = x + (jax.nn.silu(gate) * val) @ w_down.astype(val.dtype)
    return (x, k, v, logf, u_full[:, -(W_A - 1):], g_full[:, -(W_C - 1):],
            up_full[:, -(W_F - 1):])


def setup_inputs(seed: int = 0) -> dict:
    key = jax.random.key(seed)
    ks = jax.random.split(key, 32)
    f32 = jnp.float32

    def nrm(k, shape, scale):
        return jax.random.normal(k, shape, f32) * scale

    def gain(k, shape):
        return 1.0 + 0.02 * jax.random.normal(k, shape, f32)

    return {
        'x_prompt': nrm(ks[0], (BATCH, SEQ, D_MODEL), 1.0),
        'x_sample': nrm(ks[1], (DEC_BATCH, DEC_SEQ, D_MODEL), 1.0),
        'cache_k': nrm(ks[2], (DEPTH, DEC_BATCH, PAST_LEN, N_HEADS, HEAD_DIM), 1.0),
        'cache_v': nrm(ks[3], (DEPTH, DEC_BATCH, PAST_LEN, N_HEADS, HEAD_DIM), 1.0),
        'cache_logf': jax.nn.log_sigmoid(FORGET_BIAS + nrm(ks[4], (DEPTH, DEC_BATCH, PAST_LEN, N_HEADS), 1.0)),
        'state_sconv': nrm(ks[5], (DEPTH, DEC_BATCH, W_A - 1, D_A), 0.5),
        'state_cconv': nrm(ks[6], (DEPTH, DEC_BATCH, W_C - 1, D_C), 0.5),
        'state_ffn': nrm(ks[7], (DEPTH, DEC_BATCH, W_F - 1, 2 * D_FF), 1.0),
        'meta_tokens': nrm(ks[8], (N_META, D_MODEL), 1.0),
        'norm1_g': gain(ks[9], (DEPTH, D_MODEL)),
        'w_in': nrm(ks[10], (DEPTH, D_MODEL, D_IN), D_MODEL ** -0.5),
        'b_f': FORGET_BIAS + nrm(ks[11], (DEPTH, N_HEADS), 0.1),
        'q_norm_g': gain(ks[12], (DEPTH, HEAD_DIM)),
        'k_norm_g': gain(ks[13], (DEPTH, HEAD_DIM)),
        'w_conv_a': nrm(ks[14], (DEPTH, W_A, D_A), W_A ** -0.5),
        'w_conv_c': nrm(ks[15], (DEPTH, W_C, D_C), W_C ** -0.5),
        'b_conv_c': nrm(ks[16], (DEPTH, D_C), 0.02),
        'ln_c_g': gain(ks[17], (DEPTH, D_C)),
        'ln_c_b': nrm(ks[18], (DEPTH, D_C), 0.02),
        'out_g_a': gain(ks[19], (DEPTH, D_A)),
        'out_g_b': gain(ks[20], (DEPTH, D_B)),
        'out_g_c': gain(ks[21], (DEPTH, D_C)),
        'w_out': nrm(ks[22], (DEPTH, D_MIX, D_MODEL), D_MIX ** -0.5),
        'norm2_g': gain(ks[23], (DEPTH, D_MODEL)),
        'w_up': nrm(ks[24], (DEPTH, D_MODEL, 2 * D_FF), D_MODEL ** -0.5),
        'w_conv_f': nrm(ks[25], (DEPTH, W_F, 2 * D_FF), W_F ** -0.5),
        'w_down': nrm(ks[26], (DEPTH, D_FF, D_MODEL), D_FF ** -0.5),
    }


def reference(x_prompt, x_sample, cache_k, cache_v, cache_logf, state_sconv, state_cconv,
              state_ffn, meta_tokens, norm1_g, w_in, b_f, q_norm_g, k_norm_g, w_conv_a,
              w_conv_c, b_conv_c, ln_c_g, ln_c_b, out_g_a, out_g_b, out_g_c, w_out, norm2_g,
              w_up, w_conv_f, w_down):
    layer_params = (norm1_g, w_in, b_f, q_norm_g, k_norm_g, w_conv_a, w_conv_c, b_conv_c,
                    ln_c_g, ln_c_b, out_g_a, out_g_b, out_g_c, w_out, norm2_g, w_up,
                    w_conv_f, w_down)

    bp = x_prompt.shape[0]
    meta = jnp.broadcast_to(meta_tokens[None].astype(x_prompt.dtype), (bp, N_META, D_MODEL))
    xp = jnp.concatenate([meta, x_prompt], axis=1)
    kp, vp, fp, ap, cp, ffp = [], [], [], [], [], []
    for l in range(DEPTH):
        zeros_a = jnp.zeros((bp, W_A - 1, D_A), xp.dtype)
        zeros_c = jnp.zeros((bp, W_C - 1, D_C), xp.dtype)
        zeros_f = jnp.zeros((bp, W_F - 1, 2 * D_FF), xp.dtype)
        xp, k, v, lf, ba, bc, bf = layer(xp, zeros_a, zeros_c, zeros_f, fox_prompt,
                                         *[p[l] for p in layer_params])
        kp.append(k); vp.append(v); fp.append(lf); ap.append(ba); cp.append(bc); ffp.append(bf)
    y_prompt = xp[:, N_META:]

    xs = x_sample
    ks_, vs_, fs_, as_, cs_, ffs_ = [], [], [], [], [], []
    for l in range(DEPTH):
        attend = functools.partial(fox_sample, cache_k[l], cache_v[l], cache_logf[l])
        xs, k, v, lf, ba, bc, bf = layer(xs, state_sconv[l], state_cconv[l], state_ffn[l], attend,
                                         *[p[l] for p in layer_params])
        ks_.append(k); vs_.append(v); fs_.append(lf); as_.append(ba); cs_.append(bc); ffs_.append(bf)
    y_sample = xs

    new_k_prompt = jnp.stack(kp); new_v_prompt = jnp.stack(vp); new_logf_prompt = jnp.stack(fp)
    sconv_prompt = jnp.stack(ap); cconv_prompt = jnp.stack(cp); ffn_prompt = jnp.stack(ffp)
    new_k_sample = jnp.stack(ks_); new_v_sample = jnp.stack(vs_); new_logf_sample = jnp.stack(fs_)
    sconv_sample = jnp.stack(as_); cconv_sample = jnp.stack(cs_); ffn_sample = jnp.stack(ffs_)
    return (y_prompt, y_sample, new_k_prompt, new_v_prompt, new_logf_prompt, sconv_prompt,
            cconv_prompt, ffn_prompt, new_k_sample, new_v_sample, new_logf_sample, sconv_sample,
            cconv_sample, ffn_sample)
```

```python
import functools

import jax
import jax.numpy as jnp
from jax import lax
from jax.experimental import pallas as pl
from jax.experimental.pallas import tpu as pltpu

F32 = jnp.float32
BF16 = jnp.bfloat16

D_MODEL = 1024
N_META = 16
D_A = 256
D_B = 512
D_C = 256
HEAD_DIM = 64
N_HEADS = 8
N_PAIRS = N_HEADS // 2
W_A = 3
W_C = 31
W_F = 3
D_FF = 2816
EPS = 1e-6

LANES = 128
ROW_TILE = 512
TAIL = 16
HIST_A = 8
HIST_C = 32
FF_CHUNK = 256
W_IN_COLS = 3 * D_A + 3 * D_B + 2 * D_C + LANES
VMEM_LIMIT = 56 * 1024 * 1024


def _rms(x, g):
    ms = jnp.mean(x * x, axis=-1, keepdims=True)
    return x * lax.rsqrt(ms + EPS) * g


def _sigmoid(x):
    return 1.0 / (1.0 + jnp.exp(-x))


def _log_sigmoid(x):
    return jnp.minimum(x, 0.0) - jnp.log(1.0 + jnp.exp(-jnp.abs(x)))


def _dot(a, b):
    return jnp.dot(a, b, preferred_element_type=F32)


def _dot_nt(a, b):
    return lax.dot_general(a, b, (((1,), (1,)), ((), ())), preferred_element_type=F32)


def _head_norm(z, e, gain):
    sq = z * z
    hi = sq.astype(BF16)
    lo = (sq - hi.astype(F32)).astype(BF16)
    parts = []
    for c in range(D_B // 256):
        sl = slice(c * 256, (c + 1) * 256)
        parts.append(_dot(hi[:, sl], e) + _dot(lo[:, sl], e))
    ssum = jnp.concatenate(parts, axis=1)
    return z * lax.rsqrt(ssum * (1.0 / HEAD_DIM) + EPS) * gain


def _lane_scan(x, period):
    lane = lax.broadcasted_iota(jnp.int32, x.shape, 1)
    pos = lane % period
    s = 1
    while s < period:
        x = x + jnp.where(pos >= s, pltpu.roll(x, s, 1), 0.0)
        s *= 2
    return x


def _k1_rows(x, G, T, prm, ubuf, gbuf):
    M = G * T
    h = _rms(x, prm["g1"][...]).astype(BF16)
    w = prm["w"]
    o_qkv = 3 * D_A
    o_c = o_qkv + 3 * D_B
    o_f = o_c + 2 * D_C
    za = _dot(h, w[:, 0:o_qkv])
    zq = _dot(h, w[:, o_qkv:o_qkv + D_B])
    zk = _dot(h, w[:, o_qkv + D_B:o_qkv + 2 * D_B])
    zv = _dot(h, w[:, o_qkv + 2 * D_B:o_c])
    zc = _dot(h, w[:, o_c:o_f])
    zf = _dot(h, w[:, o_f:o_f + LANES])

    gb, gc, hv = za[:, 0:D_A], za[:, D_A:2 * D_A], za[:, 2 * D_A:3 * D_A]
    u3 = (gc * hv).reshape(G, T, D_A)
    ubuf[:, HIST_A:HIST_A + T, :] = u3
    wa = prm["wca"][...]
    conv = wa[2:3, :].reshape(1, 1, D_A) * u3
    for j in range(W_A - 1):
        off = HIST_A - (W_A - 1) + j
        conv = conv + wa[j:j + 1, :].reshape(1, 1, D_A) * ubuf[:, off:off + T, :]
    a_out = gb * conv.reshape(M, D_A)
    ma = _rms(a_out, prm["oga"][...]).astype(BF16)

    e = prm["e"][...]
    q = _head_norm(zq, e, prm["qg"][...])
    k = _head_norm(zk, e, prm["kg"][...])
    lf = _log_sigmoid(zf + prm["bf"][...])

    g3 = (zc[:, 0:D_C] * _sigmoid(zc[:, D_C:2 * D_C])).reshape(G, T, D_C)
    gbuf[:, HIST_C:HIST_C + T, :] = g3
    wc = prm["wcc"][...]
    cc = wc[W_C - 1:W_C, :].reshape(1, 1, D_C) * g3
    for j in range(W_C - 1):
        off = HIST_C - (W_C - 1) + j
        cc = cc + wc[j:j + 1, :].reshape(1, 1, D_C) * gbuf[:, off:off + T, :]
    cc = cc.reshape(M, D_C) + prm["bcc"][...]
    mu = jnp.mean(cc, axis=-1, keepdims=True)
    xc = cc - mu
    var = jnp.mean(xc * xc, axis=-1, keepdims=True)
    y = xc * lax.rsqrt(var + EPS) * prm["lng"][...] + prm["lnb"][...]
    c_out = y * _sigmoid(y)
    mc = _rms(c_out, prm["ogc"][...]).astype(BF16)
    return dict(ma=ma, mc=mc, q=q, k=k, v=zv, lf=lf)


def _k1_store(res, T, outs, b0):
    outs["ma"][b0, 0:T, :] = res["ma"]
    outs["mc"][b0, 0:T, :] = res["mc"]
    outs["kf"][b0, 0:T, :] = res["k"]
    outs["vf"][b0, 0:T, :] = res["v"]
    outs["lf"][b0, 0:T, :] = res["lf"][:, 0:N_HEADS]
    qb = res["q"].astype(BF16)
    kb = res["k"].astype(BF16)
    vb = res["v"].astype(BF16)
    for p in range(N_PAIRS):
        sl = slice(p * LANES, (p + 1) * LANES)
        outs["qb"][b0, p, 0:T, :] = qb[:, sl]
        outs["kb"][b0, p, 0:T, :] = kb[:, sl]
        outs["vb"][b0, p, 0:T, :] = vb[:, sl]


_K1_PARAMS = ("g1", "w", "bf", "qg", "kg", "e", "wca", "wcc", "bcc", "lng", "lnb", "oga", "ogc")
_K1_OUTS = ("ma", "mc", "qb", "kb", "vb", "kf", "vf", "lf")


def _k1_prompt_kernel(*refs):
    n_p = len(_K1_PARAMS)
    x_ref = refs[0]
    prm = dict(zip(_K1_PARAMS, refs[1:1 + n_p]))
    o = 1 + n_p
    outs = dict(zip(_K1_OUTS, refs[o:o + len(_K1_OUTS)]))
    o += len(_K1_OUTS)
    ct_ref, sao_ref, sco_ref, ubuf, gbuf, ccar = refs[o:o + 6]
    j = pl.program_id(1)
    n_full = pl.num_programs(1) - 1

    @pl.when(j == 0)
    def _():
        ubuf[:, 0:HIST_A, :] = jnp.zeros((1, HIST_A, D_A), F32)
        gbuf[:, 0:HIST_C, :] = jnp.zeros((1, HIST_C, D_C), F32)
        ccar[...] = jnp.zeros((N_HEADS, LANES), F32)

    @pl.when(j < n_full)
    def _():
        T = ROW_TILE
        res = _k1_rows(x_ref[0], 1, T, prm, ubuf, gbuf)
        _k1_store(res, T, outs, 0)
        c = _lane_scan(res["lf"].T[0:N_HEADS, :], T) + ccar[:, 0:1]
        for hd in range(N_HEADS):
            ct_ref[0, 0, hd] = c[hd:hd + 1, :]
        ccar[...] = jnp.broadcast_to(c[:, T - 1:T], (N_HEADS, LANES))
        ubuf[:, 0:HIST_A, :] = ubuf[:, T:T + HIST_A, :]
        gbuf[:, 0:HIST_C, :] = gbuf[:, T:T + HIST_C, :]

    @pl.when(j == n_full)
    def _():
        T = TAIL
        res = _k1_rows(x_ref[0, 0:T, :], 1, T, prm, ubuf, gbuf)
        _k1_store(res, T, outs, 0)
        lf_pad = jnp.concatenate([res["lf"], jnp.zeros((LANES - T, LANES), F32)], axis=0)
        c = _lane_scan(lf_pad.T[0:N_HEADS, :], LANES) + ccar[:, 0:1]
        c = jnp.concatenate([c, jnp.zeros((N_HEADS, ROW_TILE - LANES), F32)], axis=1)
        for hd in range(N_HEADS):
            ct_ref[0, 0, hd] = c[hd:hd + 1, :]
        sao_ref[0] = ubuf[0, HIST_A + T - (W_A - 1):HIST_A + T, :]
        sco_ref[0] = gbuf[0, HIST_C + T - (W_C - 1):HIST_C + T, :]


def _k1_sample_kernel(*refs):
    n_p = len(_K1_PARAMS)
    x_ref, sa_ref, sc_ref = refs[0:3]
    prm = dict(zip(_K1_PARAMS, refs[3:3 + n_p]))
    o = 3 + n_p
    outs = dict(zip(_K1_OUTS, refs[o:o + len(_K1_OUTS)]))
    o += len(_K1_OUTS)
    sao_ref, sco_ref, ubuf, gbuf = refs[o:o + 4]
    G = sa_ref.shape[0]
    T = TAIL
    ubuf[:, 0:HIST_A, :] = jnp.zeros((G, HIST_A, D_A), F32)
    gbuf[:, 0:HIST_C, :] = jnp.zeros((G, HIST_C, D_C), F32)
    ubuf[:, HIST_A - (W_A - 1):HIST_A, :] = sa_ref[...]
    gbuf[:, HIST_C - (W_C - 1):HIST_C, :] = sc_ref[...]
    res = _k1_rows(x_ref[...], G, T, prm, ubuf, gbuf)
    M = G * T
    outs["ma"][...] = res["ma"]
    outs["mc"][...] = res["mc"]
    outs["kf"][...] = res["k"]
    outs["vf"][...] = res["v"]
    outs["lf"][...] = res["lf"][:, 0:N_HEADS]
    qb = res["q"].astype(BF16)
    kb = res["k"].astype(BF16)
    vb = res["v"].astype(BF16)
    for p in range(N_PAIRS):
        sl = slice(p * LANES, (p + 1) * LANES)
        outs["qb"][p] = qb[:, sl]
        outs["kb"][p] = kb[:, sl]
        outs["vb"][p] = vb[:, sl]
    sao_ref[...] = ubuf[:, HIST_A + T - (W_A - 1):HIST_A + T, :]
    sco_ref[...] = gbuf[:, HIST_C + T - (W_C - 1):HIST_C + T, :]


def _full_spec(shape):
    n = len(shape)
    return pl.BlockSpec(shape, lambda *_: (0,) * n)


def _k1_param_specs(layer):
    small = lambda c: _full_spec((1, c))
    return [
        small(D_MODEL),
        pl.BlockSpec((None, D_MODEL, W_IN_COLS), lambda *_: (layer, 0, 0)),
        small(LANES), small(D_B), small(D_B),
        _full_spec((256, 256)),
        _full_spec((W_A, D_A)), _full_spec((W_C, D_C)),
        small(D_C), small(D_C), small(D_C), small(D_A), small(D_C),
    ]


def _k1_prompt(x, params, layer):
    B, L, _ = x.shape
    nt = pl.cdiv(L, ROW_TILE)
    row = lambda c, dt: (jax.ShapeDtypeStruct((B, L, c), dt), pl.BlockSpec((1, ROW_TILE, c), lambda b, j: (b, j, 0)))
    pair = (jax.ShapeDtypeStruct((B, N_PAIRS, L, LANES), BF16),
            pl.BlockSpec((1, N_PAIRS, ROW_TILE, LANES), lambda b, j: (b, 0, j, 0)))
    outs = [row(D_A, BF16), row(D_C, BF16), pair, pair, pair, row(D_B, F32), row(D_B, F32), row(N_HEADS, F32),
            (jax.ShapeDtypeStruct((B, nt, N_HEADS, 1, ROW_TILE), F32),
             pl.BlockSpec((1, 1, N_HEADS, 1, ROW_TILE), lambda b, j: (b, j, 0, 0, 0))),
            (jax.ShapeDtypeStruct((B, W_A - 1, D_A), F32), pl.BlockSpec((1, W_A - 1, D_A), lambda b, j: (b, 0, 0))),
            (jax.ShapeDtypeStruct((B, W_C - 1, D_C), F32), pl.BlockSpec((1, W_C - 1, D_C), lambda b, j: (b, 0, 0)))]
    return pl.pallas_call(
        _k1_prompt_kernel,
        grid=(B, nt),
        in_specs=[pl.BlockSpec((1, ROW_TILE, D_MODEL), lambda b, j: (b, j, 0))] + _k1_param_specs(layer),
        out_specs=[s for _, s in outs],
        out_shape=[s for s, _ in outs],
        scratch_shapes=[pltpu.VMEM((1, HIST_A + ROW_TILE, D_A), F32),
                        pltpu.VMEM((1, HIST_C + ROW_TILE, D_C), F32),
                        pltpu.VMEM((N_HEADS, LANES), F32)],
        compiler_params=pltpu.CompilerParams(dimension_semantics=("arbitrary", "arbitrary"),
                                             vmem_limit_bytes=VMEM_LIMIT),
        name="k1_prompt",
    )(x, *params)


def _k1_sample(x, state_a, state_c, params, layer):
    G, T, _ = x.shape
    M = G * T
    row = lambda c, dt: (jax.ShapeDtypeStruct((M, c), dt), _full_spec((M, c)))
    pair = (jax.ShapeDtypeStruct((N_PAIRS, M, LANES), BF16), _full_spec((N_PAIRS, M, LANES)))
    outs = [row(D_A, BF16), row(D_C, BF16), pair, pair, pair, row(D_B, F32), row(D_B, F32), row(N_HEADS, F32),
            (jax.ShapeDtypeStruct((G, W_A - 1, D_A), F32), _full_spec((G, W_A - 1, D_A))),
            (jax.ShapeDtypeStruct((G, W_C - 1, D_C), F32), _full_spec((G, W_C - 1, D_C)))]
    return pl.pallas_call(
        _k1_sample_kernel,
        grid=(1,),
        in_specs=[_full_spec((M, D_MODEL)), _full_spec((G, W_A - 1, D_A)), _full_spec((G, W_C - 1, D_C))]
        + _k1_param_specs(layer),
        out_specs=[s for _, s in outs],
        out_shape=[s for s, _ in outs],
        scratch_shapes=[pltpu.VMEM((G, HIST_A + T, D_A), F32), pltpu.VMEM((G, HIST_C + T, D_C), F32)],
        compiler_params=pltpu.CompilerParams(dimension_semantics=("arbitrary",), vmem_limit_bytes=VMEM_LIMIT),
        name="k1_sample",
    )(x.reshape(M, D_MODEL), state_a, state_c, *params)


def _split_heads(q2):
    lane = lax.broadcasted_iota(jnp.int32, q2.shape, 1)
    zero = jnp.zeros_like(q2)
    return jnp.concatenate([jnp.where(lane < HEAD_DIM, q2, zero), jnp.where(lane >= HEAD_DIM, q2, zero)], axis=0)


def _merge_heads(o0, o1):
    lane = lax.broadcasted_iota(jnp.int32, o0.shape, 1)
    return jnp.where(lane < HEAD_DIM, o0, o1)


def _attend_rows(qs, T, segs, kn, vn, bn):
    row = lax.broadcasted_iota(jnp.int32, (T, T), 0)
    col = lax.broadcasted_iota(jnp.int32, (T, T), 1)
    s_seg = [_dot_nt(qs, k) for k, _, _ in segs]
    s_new = _dot_nt(qs, kn)
    p_seg = [[] for _ in segs]
    p_new = []
    inv = []
    for hh in range(2):
        rs = slice(hh * T, (hh + 1) * T)
        sh = [s[rs] + b[hh] for s, (_, _, b) in zip(s_seg, segs)]
        sn = jnp.where(col <= row, s_new[rs] + bn[hh], -jnp.inf)
        m = jnp.max(sn, axis=1, keepdims=True)
        for s in sh:
            m = jnp.maximum(m, jnp.max(s, axis=1, keepdims=True))
        pn = jnp.exp(sn - m)
        l = jnp.sum(pn, axis=1, keepdims=True)
        p_new.append(pn.astype(BF16))
        for i, s in enumerate(sh):
            pe = jnp.exp(s - m)
            l = l + jnp.sum(pe, axis=1, keepdims=True)
            p_seg[i].append(pe.astype(BF16))
        inv.append(1.0 / l)
    o = _dot(jnp.concatenate(p_new, axis=0), vn)
    for i, (_, v, _) in enumerate(segs):
        o = o + _dot(jnp.concatenate(p_seg[i], axis=0), v)
    return _merge_heads(o[0:T] * inv[0], o[T:2 * T] * inv[1])


def _attn_prompt_kernel(q_ref, k_ref, v_ref, ct_ref, og_ref, o_ref, qs_ref, m_ref, l_ref, acc_ref, ob_ref):
    i = pl.program_id(1)
    n_full = pl.num_programs(1) - 1
    TQ = ROW_TILE
    TK = ROW_TILE

    @pl.when(i < n_full)
    def _():
        row = lax.broadcasted_iota(jnp.int32, (TQ, TK), 0)
        col = lax.broadcasted_iota(jnp.int32, (TQ, TK), 1)

        def pair_body(p, carry):
            qs_ref[...] = _split_heads(q_ref[0, p])
            m_ref[...] = jnp.full(m_ref.shape, -jnp.inf, F32)
            l_ref[...] = jnp.zeros(l_ref.shape, F32)
            acc_ref[...] = jnp.zeros(acc_ref.shape, F32)
            c_ref = [ct_ref[0, i, 2 * p + hh, :, 0:1] for hh in range(2)]

            def kv_step(j, masked):
                ks = pl.multiple_of(j * TK, TK)
                k2 = k_ref[0, p, pl.ds(ks, TK), :]
                v2 = v_ref[0, p, pl.ds(ks, TK), :]
                s = _dot_nt(qs_ref[...], k2)
                ps = []
                for hh in range(2):
                    rs = slice(hh * TQ, (hh + 1) * TQ)
                    sh = s[rs] + (c_ref[hh] - ct_ref[0, j, 2 * p + hh])
                    if masked:
                        sh = jnp.where(col <= row, sh, -jnp.inf)
                    m_prev = m_ref[hh]
                    m_new = jnp.maximum(m_prev, jnp.max(sh, axis=1, keepdims=True))
                    alpha = jnp.exp(m_prev - m_new)
                    pe = jnp.exp(sh - m_new[:, 0:1])
                    l_ref[hh] = alpha * l_ref[hh] + jnp.sum(pe, axis=1, keepdims=True)
                    m_ref[hh] = m_new
                    acc_ref[rs, :] = acc_ref[rs, :] * alpha
                    ps.append(pe.astype(BF16))
                acc_ref[...] += _dot(jnp.concatenate(ps, axis=0), v2)

            def loop_body(j, c):
                kv_step(j, False)
                return c

            lax.fori_loop(0, i, loop_body, 0)
            kv_step(i, True)
            ob_ref[p] = _merge_heads(acc_ref[0:TQ, :] / l_ref[0], acc_ref[TQ:2 * TQ, :] / l_ref[1])
            return carry

        lax.fori_loop(0, N_PAIRS, pair_body, 0)
        b_out = jnp.concatenate([ob_ref[p] for p in range(N_PAIRS)], axis=1)
        o_ref[0] = _rms(b_out, og_ref[...]).astype(BF16)

    @pl.when(i == n_full)
    def _():
        T = TAIL

        def pair_body(p, carry):
            qs = _split_heads(q_ref[0, p, 0:T, :])
            c_ref = [ct_ref[0, n_full, 2 * p + hh, :, 0:1] for hh in range(2)]
            segs = []
            for jj in range(k_ref.shape[2] // TK):
                sl = slice(jj * TK, (jj + 1) * TK)
                bias = [c_ref[hh] - ct_ref[0, jj, 2 * p + hh] for hh in range(2)]
                segs.append((k_ref[0, p, sl, :], v_ref[0, p, sl, :], bias))
            base = (k_ref.shape[2] // TK) * TK
            bn = [c_ref[hh] - ct_ref[0, n_full, 2 * p + hh, :, 0:T] for hh in range(2)]
            ob_ref[p, 0:T, :] = _attend_rows(qs, T, segs, k_ref[0, p, base:base + T, :],
                                             v_ref[0, p, base:base + T, :], bn)
            return carry

        lax.fori_loop(0, N_PAIRS, pair_body, 0)
        b_out = jnp.concatenate([ob_ref[p, 0:T, :] for p in range(N_PAIRS)], axis=1)
        o_ref[0, 0:T, :] = _rms(b_out, og_ref[...]).astype(BF16)


def _attn_prompt(qb, kb, vb, ct, og):
    B, _, L, _ = qb.shape
    nt = ct.shape[1]
    return pl.pallas_call(
        _attn_prompt_kernel,
        grid=(B, nt),
        in_specs=[pl.BlockSpec((1, N_PAIRS, ROW_TILE, LANES), lambda b, i: (b, 0, i, 0)),
                  pl.BlockSpec((1, N_PAIRS, L, LANES), lambda b, i: (b, 0, 0, 0)),
                  pl.BlockSpec((1, N_PAIRS, L, LANES), lambda b, i: (b, 0, 0, 0)),
                  pl.BlockSpec((1, nt, N_HEADS, 1, ROW_TILE), lambda b, i: (b, 0, 0, 0, 0)),
                  _full_spec((1, D_B))],
        out_specs=pl.BlockSpec((1, ROW_TILE, D_B), lambda b, i: (b, i, 0)),
        out_shape=jax.ShapeDtypeStruct((B, L, D_B), BF16),
        scratch_shapes=[pltpu.VMEM((2 * ROW_TILE, LANES), BF16),
                        pltpu.VMEM((2, ROW_TILE, LANES), F32),
                        pltpu.VMEM((2, ROW_TILE, LANES), F32),
                        pltpu.VMEM((2 * ROW_TILE, LANES), F32),
                        pltpu.VMEM((N_PAIRS, ROW_TILE, LANES), F32)],
        compiler_params=pltpu.CompilerParams(dimension_semantics=("arbitrary", "arbitrary"),
                                             vmem_limit_bytes=VMEM_LIMIT),
        name="attn_prompt",
    )(qb, kb, vb, ct, og)


def _attn_sample_kernel(q_ref, kn_ref, vn_ref, lf_ref, ck_ref, cv_ref, clf_ref, og_ref, o_ref):
    T = TAIL
    P = ck_ref.shape[1]
    SEG = ROW_TILE
    cc = _lane_scan(clf_ref[0], P)
    total = cc[:, P - 1:P]
    lf_pad = jnp.concatenate([lf_ref[0], jnp.zeros((T, LANES - N_HEADS), F32)], axis=1)
    lf_pad = jnp.concatenate([lf_pad, jnp.zeros((LANES - T, LANES), F32)], axis=0)
    cn = _lane_scan(lf_pad.T[0:N_HEADS, :], LANES)
    outs = []
    for p in range(N_PAIRS):
        qs = _split_heads(q_ref[p, 0])
        segs = []
        for jj in range(P // SEG):
            sl = slice(jj * SEG, (jj + 1) * SEG)
            bias = [total[2 * p + hh:2 * p + hh + 1, :] - cc[2 * p + hh:2 * p + hh + 1, sl] for hh in range(2)]
            segs.append((ck_ref[0, sl, p * LANES:(p + 1) * LANES].astype(BF16),
                         cv_ref[0, sl, p * LANES:(p + 1) * LANES].astype(BF16), bias))
        bn = [-cn[2 * p + hh:2 * p + hh + 1, 0:T] for hh in range(2)]
        outs.append(_attend_rows(qs, T, segs, kn_ref[p, 0], vn_ref[p, 0], bn))
    b_out = jnp.concatenate(outs, axis=1)
    o_ref[0] = _rms(b_out, og_ref[...]).astype(BF16)


def _attn_sample(qb, kb, vb, lf, cache_k, cache_v, cache_lft, og):
    _, G, T, _ = qb.shape
    P = cache_k.shape[1]
    pair_spec = pl.BlockSpec((N_PAIRS, 1, T, LANES), lambda b: (0, b, 0, 0))
    return pl.pallas_call(
        _attn_sample_kernel,
        grid=(G,),
        in_specs=[pair_spec, pair_spec, pair_spec,
                  pl.BlockSpec((1, T, N_HEADS), lambda b: (b, 0, 0)),
                  pl.BlockSpec((1, P, D_B), lambda b: (b, 0, 0)),
                  pl.BlockSpec((1, P, D_B), lambda b: (b, 0, 0)),
                  pl.BlockSpec((1, N_HEADS, P), lambda b: (b, 0, 0)),
                  _full_spec((1, D_B))],
        out_specs=pl.BlockSpec((1, T, D_B), lambda b: (b, 0, 0)),
        out_shape=jax.ShapeDtypeStruct((G, T, D_B), BF16),
        compiler_params=pltpu.CompilerParams(dimension_semantics=("arbitrary",), vmem_limit_bytes=VMEM_LIMIT),
        name="attn_sample",
    )(qb, kb, vb, lf, cache_k, cache_v, cache_lft, og)


def _k3_rows(x, ma, mb, mc, G, T, prm, hist, wb, ybuf):
    M = G * T
    wo = prm["wo"]
    x1 = x + _dot(ma, wo[0:D_A, :]) + _dot(mb, wo[D_A:D_A + D_B, :]) + _dot(mc, wo[D_A + D_B:D_MODEL, :])
    h2 = _rms(x1, prm["g2"][...]).astype(BF16)
    wup = prm["wup"]
    wcf = prm["wcf"]
    for c in range(D_FF // FF_CHUNK):
        halves = []
        for kk in range(2):
            off = kk * D_FF + c * FF_CHUNK
            sl = slice(off, off + FF_CHUNK)
            up3 = _dot(h2, wup[:, sl]).reshape(G, T, FF_CHUNK)
            wb[kk, :, 0:HIST_A, :] = hist[:, :, sl]
            wb[kk, :, HIST_A:HIST_A + T, :] = up3
            conv = wcf[W_F - 1:W_F, sl].reshape(1, 1, FF_CHUNK) * up3
            for j in range(W_F - 1):
                o = HIST_A - (W_F - 1) + j
                conv = conv + wcf[j:j + 1, sl].reshape(1, 1, FF_CHUNK) * wb[kk, :, o:o + T, :]
            hist[:, :, sl] = wb[kk, :, T:T + HIST_A, :]
            halves.append(conv.reshape(M, FF_CHUNK))
        gate, val = halves
        ybuf[0:M, c * FF_CHUNK:(c + 1) * FF_CHUNK] = (gate * _sigmoid(gate) * val).astype(BF16)
    return x1 + _dot(ybuf[0:M, :], prm["wdn"][...])


_K3_PARAMS = ("wo", "g2", "wup", "wcf", "wdn")


def _k3_prompt_kernel(x_ref, ma_ref, mb_ref, mc_ref, wo, g2, wup, wcf, wdn, xo_ref, fo_ref, hist, wb, ybuf):
    prm = dict(zip(_K3_PARAMS, (wo, g2, wup, wcf, wdn)))
    j = pl.program_id(1)
    n_full = pl.num_programs(1) - 1

    @pl.when(j == 0)
    def _():
        hist[...] = jnp.zeros(hist.shape, F32)

    @pl.when(j < n_full)
    def _():
        xo_ref[0] = _k3_rows(x_ref[0], ma_ref[0], mb_ref[0], mc_ref[0], 1, ROW_TILE, prm, hist, wb, ybuf)

    @pl.when(j == n_full)
    def _():
        T = TAIL
        xo_ref[0, 0:T, :] = _k3_rows(x_ref[0, 0:T, :], ma_ref[0, 0:T, :], mb_ref[0, 0:T, :], mc_ref[0, 0:T, :],
                                     1, T, prm, hist, wb, ybuf)
        fo_ref[0] = hist[0, HIST_A - (W_F - 1):HIST_A, :]


def _k3_sample_kernel(x_ref, ma_ref, mb_ref, mc_ref, sf_ref, wo, g2, wup, wcf, wdn, xo_ref, fo_ref, hist, wb, ybuf):
    prm = dict(zip(_K3_PARAMS, (wo, g2, wup, wcf, wdn)))
    G = sf_ref.shape[0]
    hist[...] = jnp.zeros(hist.shape, F32)
    hist[:, HIST_A - (W_F - 1):HIST_A, :] = sf_ref[...]
    xo_ref[...] = _k3_rows(x_ref[...], ma_ref[...], mb_ref[...], mc_ref[...], G, TAIL, prm, hist, wb, ybuf)
    fo_ref[...] = hist[:, HIST_A - (W_F - 1):HIST_A, :]


def _k3_param_specs(layer):
    return [pl.BlockSpec((None, D_MODEL, D_MODEL), lambda *_: (layer, 0, 0), pipeline_mode=pl.Buffered(1)),
            _full_spec((1, D_MODEL)),
            pl.BlockSpec((None, D_MODEL, 2 * D_FF), lambda *_: (layer, 0, 0), pipeline_mode=pl.Buffered(1)),
            _full_spec((W_F, 2 * D_FF)),
            pl.BlockSpec((None, D_FF, D_MODEL), lambda *_: (layer, 0, 0), pipeline_mode=pl.Buffered(1))]


def _k3_prompt(x, ma, mb, mc, params, layer):
    B, L, _ = x.shape
    nt = pl.cdiv(L, ROW_TILE)
    row = lambda c: pl.BlockSpec((1, ROW_TILE, c), lambda b, j: (b, j, 0))
    return pl.pallas_call(
        _k3_prompt_kernel,
        grid=(B, nt),
        in_specs=[row(D_MODEL), row(D_A), row(D_B), row(D_C)] + _k3_param_specs(layer),
        out_specs=[row(D_MODEL), pl.BlockSpec((1, W_F - 1, 2 * D_FF), lambda b, j: (b, 0, 0))],
        out_shape=[jax.ShapeDtypeStruct((B, L, D_MODEL), F32), jax.ShapeDtypeStruct((B, W_F - 1, 2 * D_FF), F32)],
        scratch_shapes=[pltpu.VMEM((1, HIST_A, 2 * D_FF), F32),
                        pltpu.VMEM((2, 1, HIST_A + ROW_TILE, FF_CHUNK), F32),
                        pltpu.VMEM((ROW_TILE, D_FF), BF16)],
        compiler_params=pltpu.CompilerParams(dimension_semantics=("arbitrary", "arbitrary"),
                                             vmem_limit_bytes=VMEM_LIMIT),
        name="k3_prompt",
    )(x, ma, mb, mc, *params)


def _k3_sample(x, ma, mb, mc, state_f, params, layer):
    G, T, _ = x.shape
    M = G * T
    return pl.pallas_call(
        _k3_sample_kernel,
        grid=(1,),
        in_specs=[_full_spec((M, D_MODEL)), _full_spec((M, D_A)), _full_spec((M, D_B)), _full_spec((M, D_C)),
                  _full_spec((G, W_F - 1, 2 * D_FF))] + _k3_param_specs(layer),
        out_specs=[_full_spec((M, D_MODEL)), _full_spec((G, W_F - 1, 2 * D_FF))],
        out_shape=[jax.ShapeDtypeStruct((M, D_MODEL), F32), jax.ShapeDtypeStruct((G, W_F - 1, 2 * D_FF), F32)],
        scratch_shapes=[pltpu.VMEM((G, HIST_A, 2 * D_FF), F32),
                        pltpu.VMEM((2, G, HIST_A + T, FF_CHUNK), F32),
                        pltpu.VMEM((M, D_FF), BF16)],
        compiler_params=pltpu.CompilerParams(dimension_semantics=("arbitrary",), vmem_limit_bytes=VMEM_LIMIT),
        name="k3_sample",
    )(x.reshape(M, D_MODEL), ma, mb.reshape(M, D_B), mc, state_f, *params)


def kernel(x_prompt, x_sample, cache_k, cache_v, cache_logf, state_sconv, state_cconv, state_ffn, meta_tokens,
           norm1_g, w_in, b_f, q_norm_g, k_norm_g, w_conv_a, w_conv_c, b_conv_c, ln_c_g, ln_c_b, out_g_a, out_g_b,
           out_g_c, w_out, norm2_g, w_up, w_conv_f, w_down):
    depth = w_in.shape[0]
    bp, seq, _ = x_prompt.shape
    G, T, _ = x_sample.shape
    P = cache_k.shape[2]

    o6 = 3 * D_A + 3 * D_B
    w_in_r = jnp.concatenate(
        [w_in[:, :, 0:o6], w_in[:, :, o6 + N_HEADS:], jnp.pad(w_in[:, :, o6:o6 + N_HEADS],
                                                               ((0, 0), (0, 0), (0, LANES - N_HEADS)))],
        axis=-1).astype(BF16)
    w_out_b = w_out.astype(BF16)
    w_up_b = w_up.astype(BF16)
    w_dn_b = w_down.astype(BF16)
    bf_pad = jnp.pad(b_f, ((0, 0), (0, LANES - N_HEADS)))
    qg = jnp.tile(q_norm_g, (1, N_HEADS)) * (HEAD_DIM ** -0.5)
    kg = jnp.tile(k_norm_g, (1, N_HEADS))
    blk = jnp.arange(256) // HEAD_DIM
    e256 = (blk[:, None] == blk[None, :]).astype(BF16)
    cache_lft = jnp.swapaxes(cache_logf, 2, 3)

    def k1_params(l):
        r = lambda a: a[l][None, :]
        return (r(norm1_g), w_in_r, r(bf_pad), r(qg), r(kg), e256, w_conv_a[l], w_conv_c[l], r(b_conv_c),
                r(ln_c_g), r(ln_c_b), r(out_g_a), r(out_g_c))

    def k3_params(l):
        return (w_out_b, norm2_g[l][None, :], w_up_b, w_conv_f[l], w_dn_b)

    meta = jnp.broadcast_to(meta_tokens[None], (bp, N_META, D_MODEL))
    xp = jnp.concatenate([meta, x_prompt], axis=1)
    L = N_META + seq
    xs = x_sample.reshape(G * T, D_MODEL)

    kp, vp, fp, ap, cp, ffp = [], [], [], [], [], []
    ks_, vs_, fs_, as_, cs_, ffs_ = [], [], [], [], [], []
    for l in range(depth):
        ma, mc, qb, kb, vb, kf, vf, lf, ct, sa, sc = _k1_prompt(xp, k1_params(l), l)
        mb = _attn_prompt(qb, kb, vb, ct, out_g_b[l][None, :])
        xp, sf = _k3_prompt(xp, ma, mb, mc, k3_params(l), l)
        kp.append(kf.reshape(bp, L, N_HEADS, HEAD_DIM)); vp.append(vf.reshape(bp, L, N_HEADS, HEAD_DIM))
        fp.append(lf); ap.append(sa); cp.append(sc); ffp.append(sf)

        ma, mc, qb, kb, vb, kf, vf, lf, sa, sc = _k1_sample(xs.reshape(G, T, D_MODEL), state_sconv[l],
                                                             state_cconv[l], k1_params(l), l)
        pr = lambda a: a.reshape(N_PAIRS, G, T, LANES)
        mb = _attn_sample(pr(qb), pr(kb), pr(vb), lf.reshape(G, T, N_HEADS), cache_k[l].reshape(G, P, D_B),
                          cache_v[l].reshape(G, P, D_B), cache_lft[l], out_g_b[l][None, :])
        xs, sf = _k3_sample(xs.reshape(G, T, D_MODEL), ma, mb, mc, state_ffn[l], k3_params(l), l)
        ks_.append(kf.reshape(G, T, N_HEADS, HEAD_DIM)); vs_.append(vf.reshape(G, T, N_HEADS, HEAD_DIM))
        fs_.append(lf.reshape(G, T, N_HEADS)); as_.append(sa); cs_.append(sc); ffs_.append(sf)

    y_prompt = xp[:, N_META:]
    y_sample = xs.reshape(G, T, D_MODEL)
    st = jnp.stack
    return (y_prompt, y_sample, st(kp), st(vp), st(fp), st(ap), st(cp), st(ffp),
            st(ks_), st(vs_), st(fs_), st(as_), st(cs_), st(ffs_))
```

```python
import jax
import jax.numpy as jnp
from jax import lax
from jax.experimental import pallas as pl
from jax.experimental.pallas import tpu as pltpu

F32 = jnp.float32
BF16 = jnp.bfloat16

D_MODEL = 1024
N_META = 16
D_A = 256
D_B = 512
D_C = 256
HEAD_DIM = 64
N_HEADS = 8
N_PAIRS = N_HEADS // 2
W_A = 3
W_C = 31
W_F = 3
D_FF = 2816
EPS = 1e-6
LOG2E = 1.4426950408889634

LANES = 128
ROW_TILE = 512
TAIL = 16
HIST_A = 8
HIST_C = 32
FF_CHUNK = 256
W_ROW_COLS = 3 * D_A + 2 * D_C + LANES
VMEM_LIMIT = 56 * 1024 * 1024


def _rms(x, g):
    ms = jnp.mean(x * x, axis=-1, keepdims=True)
    return x * lax.rsqrt(ms + EPS) * g


def _sigmoid(x):
    return 1.0 / (1.0 + jnp.exp(-x))


def _log_sigmoid(x):
    return jnp.minimum(x, 0.0) - jnp.log(1.0 + jnp.exp(-jnp.abs(x)))


def _dot(a, b):
    return jnp.dot(a, b, preferred_element_type=F32)


def _dot_nt(a, b):
    return lax.dot_general(a, b, (((1,), (1,)), ((), ())), preferred_element_type=F32)


def _head_norm_rows(z, e, gain):
    sq = z * z
    hi = sq.astype(BF16)
    lo = (sq - hi.astype(F32)).astype(BF16)
    parts = []
    for c in range(D_B // 256):
        sl = slice(c * 256, (c + 1) * 256)
        parts.append(_dot(hi[:, sl], e) + _dot(lo[:, sl], e))
    ssum = jnp.concatenate(parts, axis=1)
    return z * lax.rsqrt(ssum * (1.0 / HEAD_DIM) + EPS) * gain


def _head_norm_cols(zt, gain_b):
    t = zt.shape[1]
    z3 = zt.reshape(N_HEADS, HEAD_DIM, t)
    ss = jnp.sum(z3 * z3, axis=1, keepdims=True)
    return (z3 * lax.rsqrt(ss * (1.0 / HEAD_DIM) + EPS) * gain_b.reshape(1, HEAD_DIM, t)).reshape(D_B, t)


def _lane_scan(x, period):
    lane = lax.broadcasted_iota(jnp.int32, x.shape, 1)
    pos = lane % period
    s = 1
    while s < period:
        x = x + jnp.where(pos >= s, pltpu.roll(x, s, 1), 0.0)
        s *= 2
    return x


def _rows_to_heads(lf):
    t = lf.shape[0]
    if t < LANES:
        lf = jnp.concatenate([lf, jnp.zeros((LANES - t, LANES), F32)], axis=0)
    return lf.T[0:N_HEADS, :]


def _k1_mixers(h, G, T, prm, ubuf, gbuf):
    M = G * T
    w = prm["w"]
    o_c = 3 * D_A
    o_f = o_c + 2 * D_C
    za = _dot(h, w[:, 0:o_c])
    zc = _dot(h, w[:, o_c:o_f])
    zf = _dot(h, w[:, o_f:o_f + LANES])

    gb, gc, hv = za[:, 0:D_A], za[:, D_A:2 * D_A], za[:, 2 * D_A:3 * D_A]
    u3 = (gc * hv).reshape(G, T, D_A)
    ubuf[:, HIST_A:HIST_A + T, :] = u3
    wa = prm["wca"][...]
    conv = wa[2:3, :].reshape(1, 1, D_A) * u3
    for j in range(W_A - 1):
        off = HIST_A - (W_A - 1) + j
        conv = conv + wa[j:j + 1, :].reshape(1, 1, D_A) * ubuf[:, off:off + T, :]
    a_out = gb * conv.reshape(M, D_A)
    ma = _rms(a_out, prm["oga"][...]).astype(BF16)

    lf = _log_sigmoid(zf + prm["bf"][...])

    g3 = (zc[:, 0:D_C] * _sigmoid(zc[:, D_C:2 * D_C])).reshape(G, T, D_C)
    gbuf[:, HIST_C:HIST_C + T, :] = g3
    wc = prm["wcc"][...]
    cc = wc[W_C - 1:W_C, :].reshape(1, 1, D_C) * g3
    for j in range(W_C - 1):
        off = HIST_C - (W_C - 1) + j
        cc = cc + wc[j:j + 1, :].reshape(1, 1, D_C) * gbuf[:, off:off + T, :]
    cc = cc.reshape(M, D_C) + prm["bcc"][...]
    mu = jnp.mean(cc, axis=-1, keepdims=True)
    xc = cc - mu
    var = jnp.mean(xc * xc, axis=-1, keepdims=True)
    y = xc * lax.rsqrt(var + EPS) * prm["lng"][...] + prm["lnb"][...]
    c_out = y * _sigmoid(y)
    mc = _rms(c_out, prm["ogc"][...]).astype(BF16)
    return ma, mc, lf


_K1_PARAMS = ("g1", "w", "wqkv", "bf", "qg", "kg", "wca", "wcc", "bcc", "lng", "lnb", "oga", "ogc")


def _k1_prompt_kernel(*refs):
    n_p = len(_K1_PARAMS)
    x_ref = refs[0]
    prm = dict(zip(_K1_PARAMS, refs[1:1 + n_p]))
    (ma_ref, mc_ref, qt_ref, kp_ref, vt_ref, kf_ref, vf_ref, lft_ref, ct_ref, sao_ref, sco_ref,
     ubuf, gbuf, ccar) = refs[1 + n_p:]
    j = pl.program_id(1)
    n_full = pl.num_programs(1) - 1

    def qkv_cols(h, tw):
        z = _dot_nt(prm["wqkv"][...], h)
        qt = _head_norm_cols(z[0:D_B], prm["qg"][:, 0:tw])
        kt = _head_norm_cols(z[D_B:2 * D_B], prm["kg"][:, 0:tw])
        return qt, kt, z[2 * D_B:3 * D_B]

    @pl.when(j == 0)
    def _():
        ubuf[:, 0:HIST_A, :] = jnp.zeros((1, HIST_A, D_A), F32)
        gbuf[:, 0:HIST_C, :] = jnp.zeros((1, HIST_C, D_C), F32)
        ccar[...] = jnp.zeros((N_HEADS, LANES), F32)

    @pl.when(j < n_full)
    def _():
        T = ROW_TILE
        h = _rms(x_ref[0], prm["g1"][...]).astype(BF16)
        ma, mc, lf = _k1_mixers(h, 1, T, prm, ubuf, gbuf)
        ma_ref[0] = ma
        mc_ref[0] = mc
        qt, kt, vt = qkv_cols(h, T)
        qt_ref[0, 0] = qt.astype(BF16)
        vt_ref[0, 0] = vt.astype(BF16)
        kf_ref[0] = kt
        vf_ref[0] = vt
        kr = kt.T.astype(BF16)
        for p in range(N_PAIRS):
            kp_ref[0, p] = kr[:, p * LANES:(p + 1) * LANES]
        lft = lf.T[0:N_HEADS, :]
        lft_ref[0] = lft
        c = _lane_scan(lft, T) + ccar[:, 0:1]
        for hd in range(N_HEADS):
            ct_ref[0, 0, hd] = c[hd:hd + 1, :]
        ccar[...] = jnp.broadcast_to(c[:, T - 1:T], (N_HEADS, LANES))
        ubuf[:, 0:HIST_A, :] = ubuf[:, T:T + HIST_A, :]
        gbuf[:, 0:HIST_C, :] = gbuf[:, T:T + HIST_C, :]

    @pl.when(j == n_full)
    def _():
        T = TAIL
        h = _rms(x_ref[0, 0:T, :], prm["g1"][...]).astype(BF16)
        ma, mc, lf = _k1_mixers(h, 1, T, prm, ubuf, gbuf)
        ma_ref[0, 0:T, :] = ma
        mc_ref[0, 0:T, :] = mc
        h_pad = jnp.concatenate([h, jnp.zeros((LANES - T, D_MODEL), BF16)], axis=0)
        qt, kt, vt = qkv_cols(h_pad, LANES)
        zpad = jnp.zeros((D_B, ROW_TILE - LANES), BF16)
        qt_ref[0, 0] = jnp.concatenate([qt.astype(BF16), zpad], axis=1)
        vt_ref[0, 0] = jnp.concatenate([vt.astype(BF16), zpad], axis=1)
        kf_ref[0, :, 0:LANES] = kt
        vf_ref[0, :, 0:LANES] = vt
        kr = kt.T.astype(BF16)
        for p in range(N_PAIRS):
            kp_ref[0, p, 0:LANES, :] = kr[:, p * LANES:(p + 1) * LANES]
            kp_ref[0, p, LANES:ROW_TILE, :] = jnp.zeros((ROW_TILE - LANES, LANES), BF16)
        lft = _rows_to_heads(lf)
        lft_ref[0, :, 0:LANES] = lft
        c = _lane_scan(lft, LANES) + ccar[:, 0:1]
        c = jnp.concatenate([c, jnp.zeros((N_HEADS, ROW_TILE - LANES), F32)], axis=1)
        for hd in range(N_HEADS):
            ct_ref[0, 0, hd] = c[hd:hd + 1, :]
        sao_ref[0] = ubuf[0, HIST_A + T - (W_A - 1):HIST_A + T, :]
        sco_ref[0] = gbuf[0, HIST_C + T - (W_C - 1):HIST_C + T, :]


def _k1_sample_kernel(*refs):
    n_p = len(_K1_PARAMS)
    x_ref, sa_ref, sc_ref = refs[0:3]
    prm = dict(zip(_K1_PARAMS, refs[3:3 + n_p]))
    (ma_ref, mc_ref, q_ref, k_ref, v_ref, kf_ref, vf_ref, lf_ref, sao_ref, sco_ref, ubuf, gbuf) = refs[3 + n_p:]
    G = sa_ref.shape[0]
    T = TAIL
    ubuf[:, 0:HIST_A, :] = jnp.zeros((G, HIST_A, D_A), F32)
    gbuf[:, 0:HIST_C, :] = jnp.zeros((G, HIST_C, D_C), F32)
    ubuf[:, HIST_A - (W_A - 1):HIST_A, :] = sa_ref[...]
    gbuf[:, HIST_C - (W_C - 1):HIST_C, :] = sc_ref[...]
    h = _rms(x_ref[...], prm["g1"][...]).astype(BF16)
    ma, mc, lf = _k1_mixers(h, G, T, prm, ubuf, gbuf)
    ma_ref[...] = ma
    mc_ref[...] = mc
    lf_ref[...] = lf[:, 0:N_HEADS]
    e = prm["wqkv"]
    z = _dot(h, e[...])
    blk = lax.broadcasted_iota(jnp.int32, (256, 256), 0) // HEAD_DIM
    blk_c = lax.broadcasted_iota(jnp.int32, (256, 256), 1) // HEAD_DIM
    ones_bd = jnp.where(blk == blk_c, 1.0, 0.0).astype(BF16)
    q = _head_norm_rows(z[:, 0:D_B], ones_bd, prm["qg"][...])
    k = _head_norm_rows(z[:, D_B:2 * D_B], ones_bd, prm["kg"][...])
    v = z[:, 2 * D_B:3 * D_B]
    q_ref[...] = q.astype(BF16)
    k_ref[...] = k.astype(BF16)
    v_ref[...] = v.astype(BF16)
    kf_ref[...] = k
    vf_ref[...] = v
    sao_ref[...] = ubuf[:, HIST_A + T - (W_A - 1):HIST_A + T, :]
    sco_ref[...] = gbuf[:, HIST_C + T - (W_C - 1):HIST_C + T, :]


def _full_spec(shape):
    n = len(shape)
    return pl.BlockSpec(shape, lambda *_: (0,) * n)


def _layer_spec(shape, layer):
    n = len(shape)
    return pl.BlockSpec((None,) + tuple(shape), lambda *_: (layer,) + (0,) * n)


def _k1_param_specs(layer, wqkv_shape, gain_shape):
    small = lambda c: _full_spec((1, c))
    return [
        small(D_MODEL),
        _layer_spec((D_MODEL, W_ROW_COLS), layer),
        _layer_spec(wqkv_shape, layer),
        small(LANES), _full_spec(gain_shape), _full_spec(gain_shape),
        _full_spec((W_A, D_A)), _full_spec((W_C, D_C)),
        small(D_C), small(D_C), small(D_C), small(D_A), small(D_C),
    ]


def _k1_prompt(x, params, layer):
    B, L, _ = x.shape
    nt = pl.cdiv(L, ROW_TILE)
    row = lambda c, dt: (jax.ShapeDtypeStruct((B, L, c), dt), pl.BlockSpec((1, ROW_TILE, c), lambda b, j: (b, j, 0)))
    tile_t = (jax.ShapeDtypeStruct((B, nt, D_B, ROW_TILE), BF16),
              pl.BlockSpec((1, 1, D_B, ROW_TILE), lambda b, j: (b, j, 0, 0)))
    col = lambda r: (jax.ShapeDtypeStruct((B, r, L), F32), pl.BlockSpec((1, r, ROW_TILE), lambda b, j: (b, 0, j)))
    outs = [row(D_A, BF16), row(D_C, BF16),
            tile_t,
            (jax.ShapeDtypeStruct((B, N_PAIRS, nt * ROW_TILE, LANES), BF16),
             pl.BlockSpec((1, N_PAIRS, ROW_TILE, LANES), lambda b, j: (b, 0, j, 0))),
            tile_t,
            col(D_B), col(D_B), col(N_HEADS),
            (jax.ShapeDtypeStruct((B, nt, N_HEADS, 1, ROW_TILE), F32),
             pl.BlockSpec((1, 1, N_HEADS, 1, ROW_TILE), lambda b, j: (b, j, 0, 0, 0))),
            (jax.ShapeDtypeStruct((B, W_A - 1, D_A), F32), pl.BlockSpec((1, W_A - 1, D_A), lambda b, j: (b, 0, 0))),
            (jax.ShapeDtypeStruct((B, W_C - 1, D_C), F32), pl.BlockSpec((1, W_C - 1, D_C), lambda b, j: (b, 0, 0)))]
    return pl.pallas_call(
        _k1_prompt_kernel,
        grid=(B, nt),
        in_specs=[pl.BlockSpec((1, ROW_TILE, D_MODEL), lambda b, j: (b, j, 0))]
        + _k1_param_specs(layer, (3 * D_B, D_MODEL), (HEAD_DIM, ROW_TILE)),
        out_specs=[s for _, s in outs],
        out_shape=[s for s, _ in outs],
        scratch_shapes=[pltpu.VMEM((1, HIST_A + ROW_TILE, D_A), F32),
                        pltpu.VMEM((1, HIST_C + ROW_TILE, D_C), F32),
                        pltpu.VMEM((N_HEADS, LANES), F32)],
        compiler_params=pltpu.CompilerParams(dimension_semantics=("arbitrary", "arbitrary"),
                                             vmem_limit_bytes=VMEM_LIMIT),
        name="k1_prompt",
    )(x, *params)


def _k1_sample(x, state_a, state_c, params, layer):
    G, T, _ = x.shape
    M = G * T
    row = lambda c, dt: (jax.ShapeDtypeStruct((M, c), dt), _full_spec((M, c)))
    outs = [row(D_A, BF16), row(D_C, BF16), row(D_B, BF16), row(D_B, BF16), row(D_B, BF16),
            row(D_B, F32), row(D_B, F32), row(N_HEADS, F32),
            (jax.ShapeDtypeStruct((G, W_A - 1, D_A), F32), _full_spec((G, W_A - 1, D_A))),
            (jax.ShapeDtypeStruct((G, W_C - 1, D_C), F32), _full_spec((G, W_C - 1, D_C)))]
    return pl.pallas_call(
        _k1_sample_kernel,
        grid=(1,),
        in_specs=[_full_spec((M, D_MODEL)), _full_spec((G, W_A - 1, D_A)), _full_spec((G, W_C - 1, D_C))]
        + _k1_param_specs(layer, (D_MODEL, 3 * D_B), (1, D_B)),
        out_specs=[s for _, s in outs],
        out_shape=[s for s, _ in outs],
        scratch_shapes=[pltpu.VMEM((G, HIST_A + T, D_A), F32), pltpu.VMEM((G, HIST_C + T, D_C), F32)],
        compiler_params=pltpu.CompilerParams(dimension_semantics=("arbitrary",), vmem_limit_bytes=VMEM_LIMIT),
        name="k1_sample",
    )(x.reshape(M, D_MODEL), state_a, state_c, *params)


def _attn_prompt_kernel(qt_ref, kp_ref, vt_ref, ct_ref, og_ref, o_ref, w_ref, m_ref, l_ref, acc_ref):
    i = pl.program_id(1)
    n_full = pl.num_programs(1) - 1
    TK = ROW_TILE

    def run(TQ, t_valid):
        nc = TQ // LANES
        sub = lax.broadcasted_iota(jnp.int32, (LANES, TQ), 0)
        krow = lax.broadcasted_iota(jnp.int32, (TK, LANES), 0)
        qcol = lax.broadcasted_iota(jnp.int32, (TK, LANES), 1)
        for p in range(N_PAIRS):
            qpair = qt_ref[0, 0, p * LANES:(p + 1) * LANES, 0:TQ]
            zero = jnp.zeros_like(qpair)
            w_ref[2 * p, :, 0:TQ] = jnp.where(sub < HEAD_DIM, qpair, zero)
            w_ref[2 * p + 1, :, 0:TQ] = jnp.where(sub >= HEAD_DIM, qpair, zero)
        m_ref[:, :, 0:TQ] = jnp.full((N_HEADS, 1, TQ), -jnp.inf, F32)
        l_ref[:, :, 0:TQ] = jnp.zeros((N_HEADS, 1, TQ), F32)
        acc_ref[:, :, 0:TQ] = jnp.zeros((N_HEADS, HEAD_DIM, TQ), F32)

        def kv_step(j, masked):
            ks = pl.multiple_of(j * TK, TK)

            def scores(hd):
                return _dot(kp_ref[0, hd // 2, pl.ds(ks, TK), :], w_ref[hd, :, 0:TQ])

            st_next = scores(0)
            for hd in range(N_HEADS):
                st = st_next
                if hd + 1 < N_HEADS:
                    st_next = scores(hd + 1)
                crow = (ct_ref[0, i, hd, :, 0:1] - ct_ref[0, j, hd]) * LOG2E
                brep = jnp.broadcast_to(crow, (LANES, TK)).T
                ps = []
                alphas = []
                for c in range(nc):
                    cs = slice(c * LANES, (c + 1) * LANES)
                    sc = st[:, cs] + brep
                    if masked:
                        sc = jnp.where(krow <= qcol + c * LANES, sc, -jnp.inf)
                    m_prev = m_ref[hd, :, cs]
                    m_new = jnp.maximum(m_prev, jnp.max(sc, axis=0, keepdims=True))
                    alpha = jnp.exp2(m_prev - m_new)
                    pe = jnp.exp2(sc - m_new)
                    l_ref[hd, :, cs] = alpha * l_ref[hd, :, cs] + jnp.sum(pe, axis=0, keepdims=True)
                    m_ref[hd, :, cs] = m_new
                    ps.append(pe.astype(BF16))
                    alphas.append(alpha)
                pt = ps[0] if nc == 1 else jnp.concatenate(ps, axis=1)
                alpha = alphas[0] if nc == 1 else jnp.concatenate(alphas, axis=1)
                vh = vt_ref[0, j, hd * HEAD_DIM:(hd + 1) * HEAD_DIM, :]
                acc_ref[hd, :, 0:TQ] = acc_ref[hd, :, 0:TQ] * alpha + _dot(vh, pt)

        def loop_body(j, c):
            kv_step(j, False)
            return c

        lax.fori_loop(0, i, loop_body, 0)
        kv_step(i, True)
        o_t = jnp.concatenate([acc_ref[hd, :, 0:TQ] / l_ref[hd, :, 0:TQ] for hd in range(N_HEADS)], axis=0)
        b_out = o_t.T
        o_ref[0, 0:t_valid, :] = _rms(b_out[0:t_valid], og_ref[...]).astype(BF16)

    @pl.when(i < n_full)
    def _():
        run(ROW_TILE, ROW_TILE)

    @pl.when(i == n_full)
    def _():
        run(LANES, TAIL)


def _attn_prompt(qt, kp, vt, ct, og, L):
    B, nt = qt.shape[0], qt.shape[1]
    return pl.pallas_call(
        _attn_prompt_kernel,
        grid=(B, nt),
        in_specs=[pl.BlockSpec((1, 1, D_B, ROW_TILE), lambda b, i: (b, i, 0, 0)),
                  pl.BlockSpec((1, N_PAIRS, nt * ROW_TILE, LANES), lambda b, i: (b, 0, 0, 0)),
                  pl.BlockSpec((1, nt, D_B, ROW_TILE), lambda b, i: (b, 0, 0, 0)),
                  pl.BlockSpec((1, nt, N_HEADS, 1, ROW_TILE), lambda b, i: (b, 0, 0, 0, 0)),
                  _full_spec((1, D_B))],
        out_specs=pl.BlockSpec((1, ROW_TILE, D_B), lambda b, i: (b, i, 0)),
        out_shape=jax.ShapeDtypeStruct((B, L, D_B), BF16),
        scratch_shapes=[pltpu.VMEM((N_HEADS, LANES, ROW_TILE), BF16),
                        pltpu.VMEM((N_HEADS, 1, ROW_TILE), F32),
                        pltpu.VMEM((N_HEADS, 1, ROW_TILE), F32),
                        pltpu.VMEM((N_HEADS, HEAD_DIM, ROW_TILE), F32)],
        compiler_params=pltpu.CompilerParams(dimension_semantics=("arbitrary", "arbitrary"),
                                             vmem_limit_bytes=VMEM_LIMIT),
        name="attn_prompt",
    )(qt, kp, vt, ct, og)


def _expand_heads(b, t):
    s = b.shape[1]
    return jnp.concatenate([jnp.broadcast_to(b[hd:hd + 1, :], (t, s)) for hd in range(N_HEADS)], axis=0)


def _attn_sample_kernel(q_ref, kn_ref, vn_ref, lf_ref, ck_ref, cv_ref, clf_ref, og_ref, o_ref):
    T = TAIL
    P = ck_ref.shape[3]
    cc = _lane_scan(clf_ref[0], P)
    bias_c = (cc[:, P - 1:P] - cc) * LOG2E
    lf_pad = jnp.concatenate([lf_ref[0], jnp.zeros((T, LANES - N_HEADS), F32)], axis=1)
    cn = _lane_scan(_rows_to_heads(lf_pad), LANES)
    bias_n = -cn[:, 0:T] * LOG2E

    q = q_ref[0]
    head_of_lane = lax.broadcasted_iota(jnp.int32, (T, D_B), 1) // HEAD_DIM
    qbd = jnp.concatenate([jnp.where(head_of_lane == hd, q, jnp.zeros_like(q)) for hd in range(N_HEADS)], axis=0)
    ktb = ck_ref[0, 0].astype(BF16)
    vtb = cv_ref[0, 0].astype(BF16)
    s_c = _dot(qbd, ktb) + _expand_heads(bias_c, T)
    s_n = _dot_nt(qbd, kn_ref[0]) + _expand_heads(bias_n, T)
    row = lax.broadcasted_iota(jnp.int32, (N_HEADS * T, T), 0) % T
    col = lax.broadcasted_iota(jnp.int32, (N_HEADS * T, T), 1)
    s_n = jnp.where(col <= row, s_n, -jnp.inf)
    m = jnp.maximum(jnp.max(s_c, axis=1, keepdims=True), jnp.max(s_n, axis=1, keepdims=True))
    p_c = jnp.exp2(s_c - m)
    p_n = jnp.exp2(s_n - m)
    l = jnp.sum(p_c, axis=1, keepdims=True) + jnp.sum(p_n, axis=1, keepdims=True)
    o = (_dot_nt(p_c.astype(BF16), vtb) + _dot(p_n.astype(BF16), vn_ref[0])) / l
    b_out = jnp.zeros((T, D_B), F32)
    for hd in range(N_HEADS):
        b_out = b_out + jnp.where(head_of_lane == hd, o[hd * T:(hd + 1) * T], 0.0)
    o_ref[0] = _rms(b_out, og_ref[...]).astype(BF16)


def _attn_sample(q, kn, vn, lf, cache_kt, cache_vt, cache_lft, og, layer):
    G, T, _ = q.shape
    P = cache_kt.shape[3]
    row_spec = pl.BlockSpec((1, T, D_B), lambda b: (b, 0, 0))
    return pl.pallas_call(
        _attn_sample_kernel,
        grid=(G,),
        in_specs=[row_spec, row_spec, row_spec,
                  pl.BlockSpec((1, T, N_HEADS), lambda b: (b, 0, 0)),
                  pl.BlockSpec((1, 1, D_B, P), lambda b: (layer, b, 0, 0)),
                  pl.BlockSpec((1, 1, D_B, P), lambda b: (layer, b, 0, 0)),
                  pl.BlockSpec((None, 1, N_HEADS, P), lambda b: (layer, b, 0, 0)),
                  _full_spec((1, D_B))],
        out_specs=pl.BlockSpec((1, T, D_B), lambda b: (b, 0, 0)),
        out_shape=jax.ShapeDtypeStruct((G, T, D_B), BF16),
        compiler_params=pltpu.CompilerParams(dimension_semantics=("arbitrary",), vmem_limit_bytes=VMEM_LIMIT),
        name="attn_sample",
    )(q, kn, vn, lf, cache_kt, cache_vt, cache_lft, og)


def _k3_rows(x, ma, mb, mc, G, T, prm, hist, wb, ybuf):
    M = G * T
    wo = prm["wo"]
    x1 = x + _dot(ma, wo[0:D_A, :]) + _dot(mb, wo[D_A:D_A + D_B, :]) + _dot(mc, wo[D_A + D_B:D_MODEL, :])
    h2 = _rms(x1, prm["g2"][...]).astype(BF16)
    wup = prm["wup"]
    wcf = prm["wcf"]
    for c in range(D_FF // FF_CHUNK):
        halves = []
        for kk in range(2):
            off = kk * D_FF + c * FF_CHUNK
            sl = slice(off, off + FF_CHUNK)
            up3 = _dot(h2, wup[:, sl]).reshape(G, T, FF_CHUNK)
            wb[kk, :, 0:HIST_A, :] = hist[:, :, sl]
            wb[kk, :, HIST_A:HIST_A + T, :] = up3
            conv = wcf[W_F - 1:W_F, sl].reshape(1, 1, FF_CHUNK) * up3
            for j in range(W_F - 1):
                o = HIST_A - (W_F - 1) + j
                conv = conv + wcf[j:j + 1, sl].reshape(1, 1, FF_CHUNK) * wb[kk, :, o:o + T, :]
            hist[:, :, sl] = wb[kk, :, T:T + HIST_A, :]
            halves.append(conv.reshape(M, FF_CHUNK))
        gate, val = halves
        ybuf[0:M, c * FF_CHUNK:(c + 1) * FF_CHUNK] = (gate * _sigmoid(gate) * val).astype(BF16)
    return x1 + _dot(ybuf[0:M, :], prm["wdn"][...])


_K3_PARAMS = ("wo", "g2", "wup", "wcf", "wdn")


def _k3_prompt_kernel(x_ref, ma_ref, mb_ref, mc_ref, wo, g2, wup, wcf, wdn, xo_ref, fo_ref, hist, wb, ybuf):
    prm = dict(zip(_K3_PARAMS, (wo, g2, wup, wcf, wdn)))
    j = pl.program_id(1)
    n_full = pl.num_programs(1) - 1

    @pl.when(j == 0)
    def _():
        hist[...] = jnp.zeros(hist.shape, F32)

    @pl.when(j < n_full)
    def _():
        xo_ref[0] = _k3_rows(x_ref[0], ma_ref[0], mb_ref[0], mc_ref[0], 1, ROW_TILE, prm, hist, wb, ybuf)

    @pl.when(j == n_full)
    def _():
        T = TAIL
        xo_ref[0, 0:T, :] = _k3_rows(x_ref[0, 0:T, :], ma_ref[0, 0:T, :], mb_ref[0, 0:T, :], mc_ref[0, 0:T, :],
                                     1, T, prm, hist, wb, ybuf)
        fo_ref[0] = hist[0, HIST_A - (W_F - 1):HIST_A, :]


def _k3_sample_kernel(x_ref, ma_ref, mb_ref, mc_ref, sf_ref, wo, g2, wup, wcf, wdn, xo_ref, fo_ref, hist, wb, ybuf):
    prm = dict(zip(_K3_PARAMS, (wo, g2, wup, wcf, wdn)))
    G = sf_ref.shape[0]
    hist[...] = jnp.zeros(hist.shape, F32)
    hist[:, HIST_A - (W_F - 1):HIST_A, :] = sf_ref[...]
    xo_ref[...] = _k3_rows(x_ref[...], ma_ref[...], mb_ref[...], mc_ref[...], G, TAIL, prm, hist, wb, ybuf)
    fo_ref[...] = hist[:, HIST_A - (W_F - 1):HIST_A, :]


def _k3_param_specs(layer):
    return [pl.BlockSpec((None, D_MODEL, D_MODEL), lambda *_: (layer, 0, 0), pipeline_mode=pl.Buffered(1)),
            _full_spec((1, D_MODEL)),
            pl.BlockSpec((None, D_MODEL, 2 * D_FF), lambda *_: (layer, 0, 0), pipeline_mode=pl.Buffered(1)),
            _full_spec((W_F, 2 * D_FF)),
            pl.BlockSpec((None, D_FF, D_MODEL), lambda *_: (layer, 0, 0), pipeline_mode=pl.Buffered(1))]


def _k3_prompt(x, ma, mb, mc, params, layer):
    B, L, _ = x.shape
    nt = pl.cdiv(L, ROW_TILE)
    row = lambda c: pl.BlockSpec((1, ROW_TILE, c), lambda b, j: (b, j, 0))
    return pl.pallas_call(
        _k3_prompt_kernel,
        grid=(B, nt),
        in_specs=[row(D_MODEL), row(D_A), row(D_B), row(D_C)] + _k3_param_specs(layer),
        out_specs=[row(D_MODEL), pl.BlockSpec((1, W_F - 1, 2 * D_FF), lambda b, j: (b, 0, 0))],
        out_shape=[jax.ShapeDtypeStruct((B, L, D_MODEL), F32), jax.ShapeDtypeStruct((B, W_F - 1, 2 * D_FF), F32)],
        scratch_shapes=[pltpu.VMEM((1, HIST_A, 2 * D_FF), F32),
                        pltpu.VMEM((2, 1, HIST_A + ROW_TILE, FF_CHUNK), F32),
                        pltpu.VMEM((ROW_TILE, D_FF), BF16)],
        compiler_params=pltpu.CompilerParams(dimension_semantics=("arbitrary", "arbitrary"),
                                             vmem_limit_bytes=VMEM_LIMIT),
        name="k3_prompt",
    )(x, ma, mb, mc, *params)


def _k3_sample(x, ma, mb, mc, state_f, params, layer):
    G, T, _ = x.shape
    M = G * T
    return pl.pallas_call(
        _k3_sample_kernel,
        grid=(1,),
        in_specs=[_full_spec((M, D_MODEL)), _full_spec((M, D_A)), _full_spec((M, D_B)), _full_spec((M, D_C)),
                  _full_spec((G, W_F - 1, 2 * D_FF))] + _k3_param_specs(layer),
        out_specs=[_full_spec((M, D_MODEL)), _full_spec((G, W_F - 1, 2 * D_FF))],
        out_shape=[jax.ShapeDtypeStruct((M, D_MODEL), F32), jax.ShapeDtypeStruct((G, W_F - 1, 2 * D_FF), F32)],
        scratch_shapes=[pltpu.VMEM((G, HIST_A, 2 * D_FF), F32),
                        pltpu.VMEM((2, G, HIST_A + T, FF_CHUNK), F32),
                        pltpu.VMEM((M, D_FF), BF16)],
        compiler_params=pltpu.CompilerParams(dimension_semantics=("arbitrary",), vmem_limit_bytes=VMEM_LIMIT),
        name="k3_sample",
    )(x.reshape(M, D_MODEL), ma, mb.reshape(M, D_B), mc, state_f, *params)


def kernel(x_prompt, x_sample, cache_k, cache_v, cache_logf, state_sconv, state_cconv, state_ffn, meta_tokens,
           norm1_g, w_in, b_f, q_norm_g, k_norm_g, w_conv_a, w_conv_c, b_conv_c, ln_c_g, ln_c_b, out_g_a, out_g_b,
           out_g_c, w_out, norm2_g, w_up, w_conv_f, w_down):
    depth = w_in.shape[0]
    bp, seq, _ = x_prompt.shape
    G, T, _ = x_sample.shape
    P = cache_k.shape[2]

    o3 = 3 * D_A
    o6 = o3 + 3 * D_B
    w_fl = jnp.pad(w_in[:, :, o6:o6 + N_HEADS], ((0, 0), (0, 0), (0, LANES - N_HEADS)))
    w_rows = jnp.concatenate([w_in[:, :, 0:o3], w_in[:, :, o6 + N_HEADS:], w_fl], axis=-1).astype(BF16)
    w_qkv = w_in[:, :, o3:o6].astype(BF16)
    w_qkv_t = jnp.swapaxes(w_qkv, 1, 2)
    w_out_b = w_out.astype(BF16)
    w_up_b = w_up.astype(BF16)
    w_dn_b = w_down.astype(BF16)
    bf_pad = jnp.pad(b_f, ((0, 0), (0, LANES - N_HEADS)))
    q_scale = (HEAD_DIM ** -0.5) * LOG2E
    cache_kt = jnp.transpose(cache_k, (0, 1, 3, 4, 2)).reshape(depth, G, D_B, P)
    cache_vt = jnp.transpose(cache_v, (0, 1, 3, 4, 2)).reshape(depth, G, D_B, P)
    cache_lft = jnp.swapaxes(cache_logf, 2, 3)

    def k1_params(l, cols):
        r = lambda a: a[l][None, :]
        if cols:
            qg = jnp.broadcast_to((q_norm_g[l] * q_scale)[:, None], (HEAD_DIM, ROW_TILE))
            kg = jnp.broadcast_to(k_norm_g[l][:, None], (HEAD_DIM, ROW_TILE))
            wqkv = w_qkv_t
        else:
            qg = jnp.tile(q_norm_g[l] * q_scale, N_HEADS)[None, :]
            kg = jnp.tile(k_norm_g[l], N_HEADS)[None, :]
            wqkv = w_qkv
        return (r(norm1_g), w_rows, wqkv, r(bf_pad), qg, kg, w_conv_a[l], w_conv_c[l], r(b_conv_c),
                r(ln_c_g), r(ln_c_b), r(out_g_a), r(out_g_c))

    def k3_params(l):
        return (w_out_b, norm2_g[l][None, :], w_up_b, w_conv_f[l], w_dn_b)

    meta = jnp.broadcast_to(meta_tokens[None], (bp, N_META, D_MODEL))
    xp = jnp.concatenate([meta, x_prompt], axis=1)
    L = N_META + seq
    xs = x_sample.reshape(G * T, D_MODEL)

    kp_, vp_, fp_, ap_, cp_, ffp_ = [], [], [], [], [], []
    ks_, vs_, fs_, as_, cs_, ffs_ = [], [], [], [], [], []
    for l in range(depth):
        ma, mc, qt, kpair, vt, kf, vf, lft, ct, sa, sc = _k1_prompt(xp, k1_params(l, True), l)
        mb = _attn_prompt(qt, kpair, vt, ct, out_g_b[l][None, :], L)
        xp, sf = _k3_prompt(xp, ma, mb, mc, k3_params(l), l)
        to_out = lambda a: jnp.transpose(a.reshape(bp, N_HEADS, HEAD_DIM, L), (0, 3, 1, 2))
        kp_.append(to_out(kf)); vp_.append(to_out(vf)); fp_.append(jnp.swapaxes(lft, 1, 2))
        ap_.append(sa); cp_.append(sc); ffp_.append(sf)

        ma, mc, q, k, v, kf, vf, lf, sa, sc = _k1_sample(xs.reshape(G, T, D_MODEL), state_sconv[l],
                                                         state_cconv[l], k1_params(l, False), l)
        r3 = lambda a: a.reshape(G, T, D_B)
        mb = _attn_sample(r3(q), r3(k), r3(v), lf.reshape(G, T, N_HEADS), cache_kt, cache_vt, cache_lft,
                          out_g_b[l][None, :], l)
        xs, sf = _k3_sample(xs.reshape(G, T, D_MODEL), ma, mb, mc, state_ffn[l], k3_params(l), l)
        ks_.append(kf.reshape(G, T, N_HEADS, HEAD_DIM)); vs_.append(vf.reshape(G, T, N_HEADS, HEAD_DIM))
        fs_.append(lf.reshape(G, T, N_HEADS)); as_.append(sa); cs_.append(sc); ffs_.append(sf)

    y_prompt = xp[:, N_META:]
    y_sample = xs.reshape(G, T, D_MODEL)
    st = jnp.stack
    return (y_prompt, y_sample, st(kp_), st(vp_), st(fp_), st(ap_), st(cp_), st(ffp_),
            st(ks_), st(vs_), st(fs_), st(as_), st(cs_), st(ffs_))
```

```python
import jax
import jax.numpy as jnp
from jax import lax
from jax.experimental import pallas as pl
from jax.experimental.pallas import tpu as pltpu

F32 = jnp.float32
BF16 = jnp.bfloat16

D_MODEL = 1024
N_META = 16
D_A = 256
D_B = 512
D_C = 256
HEAD_DIM = 64
N_HEADS = 8
W_A = 3
W_C = 31
W_F = 3
D_FF = 2816
EPS = 1e-6
LOG2E = 1.4426950408889634

LANES = 128
ROW_TILE = 512
TAIL = 16
HIST_A = 8
HIST_C = 32
FF_CHUNK = 256
W_ROW_COLS = 3 * D_A + 2 * D_C + LANES
VMEM_LIMIT = 56 * 1024 * 1024


def _rms(x, g):
    ms = jnp.mean(x * x, axis=-1, keepdims=True)
    return x * lax.rsqrt(ms + EPS) * g


def _sigmoid(x):
    return 0.5 + 0.5 * jnp.tanh(0.5 * x)


def _log_sigmoid(x):
    return jnp.minimum(x, 0.0) - jnp.log(1.0 + jnp.exp(-jnp.abs(x)))


def _dot(a, b):
    return jnp.dot(a, b, preferred_element_type=F32)


def _dot_nt(a, b):
    return lax.dot_general(a, b, (((1,), (1,)), ((), ())), preferred_element_type=F32)


def _head_norm_rows(z, e, gain):
    sq = z * z
    hi = sq.astype(BF16)
    lo = (sq - hi.astype(F32)).astype(BF16)
    parts = []
    for c in range(D_B // 256):
        sl = slice(c * 256, (c + 1) * 256)
        parts.append(_dot(hi[:, sl], e) + _dot(lo[:, sl], e))
    ssum = jnp.concatenate(parts, axis=1)
    return z * lax.rsqrt(ssum * (1.0 / HEAD_DIM) + EPS) * gain


def _head_norm_cols(zt, gain_b):
    t = zt.shape[1]
    z3 = zt.reshape(N_HEADS, HEAD_DIM, t)
    ss = jnp.sum(z3 * z3, axis=1, keepdims=True)
    return (z3 * lax.rsqrt(ss * (1.0 / HEAD_DIM) + EPS) * gain_b.reshape(1, HEAD_DIM, t)).reshape(D_B, t)


def _lane_scan(x, period):
    lane = lax.broadcasted_iota(jnp.int32, x.shape, 1)
    pos = lane % period
    s = 1
    while s < period:
        x = x + jnp.where(pos >= s, pltpu.roll(x, s, 1), 0.0)
        s *= 2
    return x


def _rows_to_heads(lf):
    t = lf.shape[0]
    if t < LANES:
        lf = jnp.concatenate([lf, jnp.zeros((LANES - t, LANES), F32)], axis=0)
    return lf.T[0:N_HEADS, :]


def _k1_row_dots(h, prm):
    w = prm["w"]
    o_c = 3 * D_A
    o_f = o_c + 2 * D_C
    return _dot(h, w[:, 0:o_c]), _dot(h, w[:, o_c:o_f]), _dot(h, w[:, o_f:o_f + LANES])


def _k1_mixers(zs, G, T, prm, ubuf, gbuf, sbuf, hooks=(None, None)):
    M = G * T
    za, zc, zf = zs

    gb, gc, hv = za[:, 0:D_A], za[:, D_A:2 * D_A], za[:, 2 * D_A:3 * D_A]
    u3 = (gc * hv).reshape(G, T, D_A)
    ubuf[:, HIST_A:HIST_A + T, :] = u3
    wa = prm["wca"][...]
    conv = wa[2:3, :].reshape(1, 1, D_A) * u3
    for j in range(W_A - 1):
        off = HIST_A - (W_A - 1) + j
        conv = conv + wa[j:j + 1, :].reshape(1, 1, D_A) * ubuf[:, off:off + T, :]
    a_out = gb * conv.reshape(M, D_A)
    ma = _rms(a_out, prm["oga"][...]).astype(BF16)

    if hooks[0] is not None:
        hooks[0]()

    g3 = (zc[:, 0:D_C] * _sigmoid(zc[:, D_C:2 * D_C])).reshape(G, T, D_C)
    gbuf[:, HIST_C:HIST_C + T, :] = g3
    wc = prm["wcc"][...]
    cc = None
    first = HIST_C - (W_C - 1)
    for ph in range(8):
        taps = [(a, 8 * a + ph - first) for a in range((W_C + first) // 8 + 1) if 0 <= 8 * a + ph - first < W_C]
        if not taps:
            continue
        n = T + 8 * taps[-1][0]
        if ph == 0:
            src = gbuf
        else:
            sbuf[ph - 1, :, 0:n, :] = gbuf[:, ph:ph + n, :]
            src = sbuf.at[ph - 1]
        for a, j in taps:
            term = wc[j:j + 1, :].reshape(1, 1, D_C) * src[:, 8 * a:8 * a + T, :]
            cc = term if cc is None else cc + term
        if ph == 3 and hooks[1] is not None:
            hooks[1]()
    cc = cc.reshape(M, D_C) + prm["bcc"][...]
    mu = jnp.mean(cc, axis=-1, keepdims=True)
    xc = cc - mu
    var = jnp.mean(xc * xc, axis=-1, keepdims=True)
    y = xc * lax.rsqrt(var + EPS) * prm["lng"][...] + prm["lnb"][...]
    c_out = y * _sigmoid(y)
    mc = _rms(c_out, prm["ogc"][...]).astype(BF16)
    return ma, mc


def _forget_gates(zf, prm):
    return _log_sigmoid(zf + prm["bf"][...])


def _key_slabs(kr, c):
    r = kr.shape[0]
    brel = (c[:, 0:1] - c) * LOG2E
    bcol = jnp.concatenate([brel, jnp.zeros((LANES - N_HEADS, r), F32)], axis=0).T
    hi = bcol.astype(BF16).astype(F32)
    rest = bcol - hi
    mid = rest.astype(BF16).astype(F32)
    lo = (rest - mid).astype(BF16).astype(F32)
    c3 = hi + pltpu.roll(mid, 8, 1) + pltpu.roll(lo, 16, 1)
    lane = lax.broadcasted_iota(jnp.int32, (r, LANES), 1)
    is_bias = (lane == HEAD_DIM) | (lane == HEAD_DIM + 8) | (lane == HEAD_DIM + 16)
    slabs = []
    for hd in range(N_HEADS):
        pair = kr[:, (hd // 2) * LANES:(hd // 2 + 1) * LANES]
        kpart = pair if hd % 2 == 0 else pltpu.roll(pair, HEAD_DIM, 1)
        aug = pltpu.roll(c3, HEAD_DIM - hd, 1)
        slabs.append(jnp.where(lane < HEAD_DIM, kpart, jnp.where(is_bias, aug, 0.0)).astype(BF16))
    return slabs


_K1_PARAMS = ("g1", "w", "wqkv", "bf", "qg", "kg", "wca", "wcc", "bcc", "lng", "lnb", "oga", "ogc")


def _k1_prompt_kernel(*refs):
    n_p = len(_K1_PARAMS)
    x_ref = refs[0]
    prm = dict(zip(_K1_PARAMS, refs[1:1 + n_p]))
    (ma_ref, mc_ref, qt_ref, kh_ref, vt_ref, kf_ref, vf_ref, lft_ref, ct_ref, sao_ref, sco_ref,
     ubuf, gbuf, sbuf, ccar) = refs[1 + n_p:]
    j = pl.program_id(1)
    n_full = pl.num_programs(1) - 1

    def proj_cols(h, part):
        return _dot_nt(prm["wqkv"][part * D_B:(part + 1) * D_B, :], h)

    @pl.when(j == 0)
    def _():
        ubuf[:, 0:HIST_A, :] = jnp.zeros((1, HIST_A, D_A), F32)
        gbuf[:, 0:HIST_C, :] = jnp.zeros((1, HIST_C, D_C), F32)
        ccar[...] = jnp.zeros((N_HEADS, LANES), F32)

    @pl.when(j < n_full)
    def _():
        T = ROW_TILE
        h = _rms(x_ref[0], prm["g1"][...]).astype(BF16)
        zs = _k1_row_dots(h, prm)
        z = {"q": proj_cols(h, 0)}
        lft = _forget_gates(zs[2], prm).T[0:N_HEADS, :]
        lft_ref[0] = lft
        c = _lane_scan(lft, T) + ccar[:, 0:1]
        for hd in range(N_HEADS):
            ct_ref[0, 0, hd] = c[hd:hd + 1, :]
        ccar[...] = jnp.broadcast_to(c[:, T - 1:T], (N_HEADS, LANES))

        def after_mixer_a():
            qt_ref[0, 0] = _head_norm_cols(z["q"], prm["qg"][...]).astype(BF16)
            z["k"] = proj_cols(h, 1)

        def mid_conv():
            kt = _head_norm_cols(z["k"], prm["kg"][...])
            kf_ref[0] = kt
            for hd, slab in enumerate(_key_slabs(kt.T, c)):
                kh_ref[0, hd] = slab
            z["v"] = proj_cols(h, 2)

        ma, mc = _k1_mixers(zs, 1, T, prm, ubuf, gbuf, sbuf, (after_mixer_a, mid_conv))
        ma_ref[0] = ma
        mc_ref[0] = mc
        vf_ref[0] = z["v"]
        vt_ref[0, 0] = z["v"].astype(BF16)
        ubuf[:, 0:HIST_A, :] = ubuf[:, T:T + HIST_A, :]
        gbuf[:, 0:HIST_C, :] = gbuf[:, T:T + HIST_C, :]

    @pl.when(j == n_full)
    def _():
        T = TAIL
        h = _rms(x_ref[0, 0:T, :], prm["g1"][...]).astype(BF16)
        zs = _k1_row_dots(h, prm)
        ma, mc = _k1_mixers(zs, 1, T, prm, ubuf, gbuf, sbuf)
        ma_ref[0, 0:T, :] = ma
        mc_ref[0, 0:T, :] = mc
        h_pad = jnp.concatenate([h, jnp.zeros((LANES - T, D_MODEL), BF16)], axis=0)
        qt = _head_norm_cols(proj_cols(h_pad, 0), prm["qg"][:, 0:LANES])
        kt = _head_norm_cols(proj_cols(h_pad, 1), prm["kg"][:, 0:LANES])
        vt = proj_cols(h_pad, 2)
        zpad = jnp.zeros((D_B, ROW_TILE - LANES), BF16)
        qt_ref[0, 0] = jnp.concatenate([qt.astype(BF16), zpad], axis=1)
        vt_ref[0, 0] = jnp.concatenate([vt.astype(BF16), zpad], axis=1)
        kf_ref[0, :, 0:LANES] = kt
        vf_ref[0, :, 0:LANES] = vt
        lft = _rows_to_heads(_forget_gates(zs[2], prm))
        lft_ref[0, :, 0:LANES] = lft
        c = _lane_scan(lft, LANES) + ccar[:, 0:1]
        for hd, slab in enumerate(_key_slabs(kt.T, c)):
            kh_ref[0, hd, 0:LANES, :] = slab
            kh_ref[0, hd, LANES:ROW_TILE, :] = jnp.zeros((ROW_TILE - LANES, LANES), BF16)
        c = jnp.concatenate([c, jnp.zeros((N_HEADS, ROW_TILE - LANES), F32)], axis=1)
        for hd in range(N_HEADS):
            ct_ref[0, 0, hd] = c[hd:hd + 1, :]
        sao_ref[0] = ubuf[0, HIST_A + T - (W_A - 1):HIST_A + T, :]
        sco_ref[0] = gbuf[0, HIST_C + T - (W_C - 1):HIST_C + T, :]


def _k1_sample_kernel(*refs):
    n_p = len(_K1_PARAMS)
    x_ref, sa_ref, sc_ref = refs[0:3]
    prm = dict(zip(_K1_PARAMS, refs[3:3 + n_p]))
    (ma_ref, mc_ref, q_ref, k_ref, v_ref, kf_ref, vf_ref, lf_ref, sao_ref, sco_ref, ubuf, gbuf, sbuf) = refs[3 + n_p:]
    G = sa_ref.shape[0]
    T = TAIL
    ubuf[:, 0:HIST_A, :] = jnp.zeros((G, HIST_A, D_A), F32)
    gbuf[:, 0:HIST_C, :] = jnp.zeros((G, HIST_C, D_C), F32)
    ubuf[:, HIST_A - (W_A - 1):HIST_A, :] = sa_ref[...]
    gbuf[:, HIST_C - (W_C - 1):HIST_C, :] = sc_ref[...]
    h = _rms(x_ref[...], prm["g1"][...]).astype(BF16)
    zs = _k1_row_dots(h, prm)
    ma, mc = _k1_mixers(zs, G, T, prm, ubuf, gbuf, sbuf)
    ma_ref[...] = ma
    mc_ref[...] = mc
    lf_ref[...] = _forget_gates(zs[2], prm)[:, 0:N_HEADS]
    e = prm["wqkv"]
    z = _dot(h, e[...])
    blk = lax.broadcasted_iota(jnp.int32, (256, 256), 0) // HEAD_DIM
    blk_c = lax.broadcasted_iota(jnp.int32, (256, 256), 1) // HEAD_DIM
    ones_bd = jnp.where(blk == blk_c, 1.0, 0.0).astype(BF16)
    q = _head_norm_rows(z[:, 0:D_B], ones_bd, prm["qg"][...])
    k = _head_norm_rows(z[:, D_B:2 * D_B], ones_bd, prm["kg"][...])
    v = z[:, 2 * D_B:3 * D_B]
    q_ref[...] = q.astype(BF16)
    k_ref[...] = k.astype(BF16)
    v_ref[...] = v.astype(BF16)
    kf_ref[...] = k
    vf_ref[...] = v
    sao_ref[...] = ubuf[:, HIST_A + T - (W_A - 1):HIST_A + T, :]
    sco_ref[...] = gbuf[:, HIST_C + T - (W_C - 1):HIST_C + T, :]


def _full_spec(shape):
    n = len(shape)
    return pl.BlockSpec(shape, lambda *_: (0,) * n)


def _layer_spec(shape, layer):
    n = len(shape)
    return pl.BlockSpec((None,) + tuple(shape), lambda *_: (layer,) + (0,) * n)


def _k1_param_specs(layer, wqkv_shape, gain_shape):
    small = lambda c: _full_spec((1, c))
    return [
        small(D_MODEL),
        _layer_spec((D_MODEL, W_ROW_COLS), layer),
        _layer_spec(wqkv_shape, layer),
        small(LANES), _full_spec(gain_shape), _full_spec(gain_shape),
        _full_spec((W_A, D_A)), _full_spec((W_C, D_C)),
        small(D_C), small(D_C), small(D_C), small(D_A), small(D_C),
    ]


def _k1_prompt(x, params, layer):
    B, L, _ = x.shape
    nt = pl.cdiv(L, ROW_TILE)
    row = lambda c, dt: (jax.ShapeDtypeStruct((B, L, c), dt), pl.BlockSpec((1, ROW_TILE, c), lambda b, j: (b, j, 0)))
    tile_t = (jax.ShapeDtypeStruct((B, nt, D_B, ROW_TILE), BF16),
              pl.BlockSpec((1, 1, D_B, ROW_TILE), lambda b, j: (b, j, 0, 0)))
    col = lambda r: (jax.ShapeDtypeStruct((B, r, L), F32), pl.BlockSpec((1, r, ROW_TILE), lambda b, j: (b, 0, j)))
    outs = [row(D_A, BF16), row(D_C, BF16),
            tile_t,
            (jax.ShapeDtypeStruct((B, N_HEADS, nt * ROW_TILE, LANES), BF16),
             pl.BlockSpec((1, N_HEADS, ROW_TILE, LANES), lambda b, j: (b, 0, j, 0))),
            tile_t,
            col(D_B), col(D_B), col(N_HEADS),
            (jax.ShapeDtypeStruct((B, nt, N_HEADS, 1, ROW_TILE), F32),
             pl.BlockSpec((1, 1, N_HEADS, 1, ROW_TILE), lambda b, j: (b, j, 0, 0, 0))),
            (jax.ShapeDtypeStruct((B, W_A - 1, D_A), F32), pl.BlockSpec((1, W_A - 1, D_A), lambda b, j: (b, 0, 0))),
            (jax.ShapeDtypeStruct((B, W_C - 1, D_C), F32), pl.BlockSpec((1, W_C - 1, D_C), lambda b, j: (b, 0, 0)))]
    return pl.pallas_call(
        _k1_prompt_kernel,
        grid=(B, nt),
        in_specs=[pl.BlockSpec((1, ROW_TILE, D_MODEL), lambda b, j: (b, j, 0))]
        + _k1_param_specs(layer, (3 * D_B, D_MODEL), (HEAD_DIM, ROW_TILE)),
        out_specs=[s for _, s in outs],
        out_shape=[s for s, _ in outs],
        scratch_shapes=[pltpu.VMEM((1, HIST_A + ROW_TILE, D_A), F32),
                        pltpu.VMEM((1, HIST_C + ROW_TILE, D_C), F32),
                        pltpu.VMEM((7, 1, HIST_C + ROW_TILE, D_C), F32),
                        pltpu.VMEM((N_HEADS, LANES), F32)],
        compiler_params=pltpu.CompilerParams(dimension_semantics=("arbitrary", "arbitrary"),
                                             vmem_limit_bytes=VMEM_LIMIT),
        name="k1_prompt",
    )(x, *params)


def _k1_sample(x, state_a, state_c, params, layer):
    G, T, _ = x.shape
    M = G * T
    row = lambda c, dt: (jax.ShapeDtypeStruct((M, c), dt), _full_spec((M, c)))
    outs = [row(D_A, BF16), row(D_C, BF16), row(D_B, BF16), row(D_B, BF16), row(D_B, BF16),
            row(D_B, F32), row(D_B, F32), row(N_HEADS, F32),
            (jax.ShapeDtypeStruct((G, W_A - 1, D_A), F32), _full_spec((G, W_A - 1, D_A))),
            (jax.ShapeDtypeStruct((G, W_C - 1, D_C), F32), _full_spec((G, W_C - 1, D_C)))]
    return pl.pallas_call(
        _k1_sample_kernel,
        grid=(1,),
        in_specs=[_full_spec((M, D_MODEL)), _full_spec((G, W_A - 1, D_A)), _full_spec((G, W_C - 1, D_C))]
        + _k1_param_specs(layer, (D_MODEL, 3 * D_B), (1, D_B)),
        out_specs=[s for _, s in outs],
        out_shape=[s for s, _ in outs],
        scratch_shapes=[pltpu.VMEM((G, HIST_A + T, D_A), F32), pltpu.VMEM((G, HIST_C + T, D_C), F32),
                        pltpu.VMEM((7, G, HIST_C + T, D_C), F32)],
        compiler_params=pltpu.CompilerParams(dimension_semantics=("arbitrary",), vmem_limit_bytes=VMEM_LIMIT),
        name="k1_sample",
    )(x.reshape(M, D_MODEL), state_a, state_c, *params)


def _attn_prompt_kernel(qt_ref, kh_ref, vt_ref, ct_ref, og_ref, o_ref, w_ref, m_ref, l_ref, acc_ref):
    i = pl.program_id(1)
    n_full = pl.num_programs(1) - 1
    TK = ROW_TILE

    def run(TQ, t_valid):
        nc = TQ // LANES
        sub = lax.broadcasted_iota(jnp.int32, (HEAD_DIM, TQ), 0)
        ones_rows = jnp.where((sub == 0) | (sub == 8) | (sub == 16), 1.0, 0.0).astype(BF16)
        for hd in range(N_HEADS):
            w_ref[hd, :, 0:TQ] = jnp.concatenate(
                [qt_ref[0, 0, hd * HEAD_DIM:(hd + 1) * HEAD_DIM, 0:TQ], ones_rows], axis=0)
        m_ref[:, :, 0:TQ] = jnp.full((N_HEADS, 1, TQ), -jnp.inf, F32)
        l_ref[:, :, 0:TQ] = jnp.zeros((N_HEADS, 1, TQ), F32)
        acc_ref[:, :, 0:TQ] = jnp.zeros((N_HEADS, HEAD_DIM, TQ), F32)

        def kv_step(j, masked):
            ks = pl.multiple_of(j * TK, TK)

            def scores(hd):
                return _dot(kh_ref[0, hd, pl.ds(ks, TK), :], w_ref[hd, :, 0:TQ])

            ahead = 3
            pending = [scores(hd) for hd in range(ahead)]
            for hd in range(N_HEADS):
                st = pending.pop(0)
                if hd + ahead < N_HEADS:
                    pending.append(scores(hd + ahead))
                d = (ct_ref[0, i, hd, :, 0:1] - ct_ref[0, j, hd, :, 0:1]) * LOG2E
                ps = []
                alphas = []
                for c in range(nc):
                    cs = slice(c * LANES, (c + 1) * LANES)
                    rv = min(TK, (c + 1) * LANES) if masked else TK
                    sc = st[0:rv, cs]
                    if masked:
                        krow = lax.broadcasted_iota(jnp.int32, (rv, LANES), 0)
                        qcol = lax.broadcasted_iota(jnp.int32, (rv, LANES), 1)
                        sc = jnp.where(krow <= qcol + c * LANES, sc, -jnp.inf)
                    m_prev = m_ref[hd, :, cs]
                    m_new = jnp.maximum(m_prev, jnp.max(sc, axis=0, keepdims=True) + d)
                    alpha = jnp.exp2(m_prev - m_new)
                    pe = jnp.exp2(sc - (m_new - d))
                    l_ref[hd, :, cs] = alpha * l_ref[hd, :, cs] + jnp.sum(pe, axis=0, keepdims=True)
                    m_ref[hd, :, cs] = m_new
                    pb = pe.astype(BF16)
                    if rv < TK:
                        pb = jnp.concatenate([pb, jnp.zeros((TK - rv, LANES), BF16)], axis=0)
                    ps.append(pb)
                    alphas.append(alpha)
                pt = ps[0] if nc == 1 else jnp.concatenate(ps, axis=1)
                alpha = alphas[0] if nc == 1 else jnp.concatenate(alphas, axis=1)
                vh = vt_ref[0, j, hd * HEAD_DIM:(hd + 1) * HEAD_DIM, :]
                acc_ref[hd, :, 0:TQ] = acc_ref[hd, :, 0:TQ] * alpha + _dot(vh, pt)

        def loop_body(j, c):
            kv_step(j, False)
            return c

        lax.fori_loop(0, i, loop_body, 0)
        kv_step(i, True)
        o_t = jnp.concatenate([acc_ref[hd, :, 0:TQ] / l_ref[hd, :, 0:TQ] for hd in range(N_HEADS)], axis=0)
        b_out = o_t.T
        o_ref[0, 0:t_valid, :] = _rms(b_out[0:t_valid], og_ref[...]).astype(BF16)

    @pl.when(i < n_full)
    def _():
        run(ROW_TILE, ROW_TILE)

    @pl.when(i == n_full)
    def _():
        run(LANES, TAIL)


def _attn_prompt(qt, kp, vt, ct, og, L):
    B, nt = qt.shape[0], qt.shape[1]
    return pl.pallas_call(
        _attn_prompt_kernel,
        grid=(B, nt),
        in_specs=[pl.BlockSpec((1, 1, D_B, ROW_TILE), lambda b, i: (b, i, 0, 0)),
                  pl.BlockSpec((1, N_HEADS, nt * ROW_TILE, LANES), lambda b, i: (b, 0, 0, 0)),
                  pl.BlockSpec((1, nt, D_B, ROW_TILE), lambda b, i: (b, 0, 0, 0)),
                  pl.BlockSpec((1, nt, N_HEADS, 1, ROW_TILE), lambda b, i: (b, 0, 0, 0, 0)),
                  _full_spec((1, D_B))],
        out_specs=pl.BlockSpec((1, ROW_TILE, D_B), lambda b, i: (b, i, 0)),
        out_shape=jax.ShapeDtypeStruct((B, L, D_B), BF16),
        scratch_shapes=[pltpu.VMEM((N_HEADS, LANES, ROW_TILE), BF16),
                        pltpu.VMEM((N_HEADS, 1, ROW_TILE), F32),
                        pltpu.VMEM((N_HEADS, 1, ROW_TILE), F32),
                        pltpu.VMEM((N_HEADS, HEAD_DIM, ROW_TILE), F32)],
        compiler_params=pltpu.CompilerParams(dimension_semantics=("arbitrary", "arbitrary"),
                                             vmem_limit_bytes=VMEM_LIMIT),
        name="attn_prompt",
    )(qt, kp, vt, ct, og)


def _expand_heads(b, t):
    s = b.shape[1]
    return jnp.concatenate([jnp.broadcast_to(b[hd:hd + 1, :], (t, s)) for hd in range(N_HEADS)], axis=0)


def _attn_sample_kernel(q_ref, kn_ref, vn_ref, lf_ref, ck_ref, cv_ref, clf_ref, og_ref, o_ref):
    T = TAIL
    P = ck_ref.shape[3]
    cc = _lane_scan(clf_ref[0], P)
    bias_c = (cc[:, P - 1:P] - cc) * LOG2E
    lf_pad = jnp.concatenate([lf_ref[0], jnp.zeros((T, LANES - N_HEADS), F32)], axis=1)
    cn = _lane_scan(_rows_to_heads(lf_pad), LANES)
    bias_n = -cn[:, 0:T] * LOG2E

    q = q_ref[0]
    head_of_lane = lax.broadcasted_iota(jnp.int32, (T, D_B), 1) // HEAD_DIM
    qbd = jnp.concatenate([jnp.where(head_of_lane == hd, q, jnp.zeros_like(q)) for hd in range(N_HEADS)], axis=0)
    ktb = ck_ref[0, 0].astype(BF16)
    vtb = cv_ref[0, 0].astype(BF16)
    s_c = _dot(qbd, ktb) + _expand_heads(bias_c, T)
    s_n = _dot_nt(qbd, kn_ref[0]) + _expand_heads(bias_n, T)
    row = lax.broadcasted_iota(jnp.int32, (N_HEADS * T, T), 0) % T
    col = lax.broadcasted_iota(jnp.int32, (N_HEADS * T, T), 1)
    s_n = jnp.where(col <= row, s_n, -jnp.inf)
    m = jnp.maximum(jnp.max(s_c, axis=1, keepdims=True), jnp.max(s_n, axis=1, keepdims=True))
    p_c = jnp.exp2(s_c - m)
    p_n = jnp.exp2(s_n - m)
    l = jnp.sum(p_c, axis=1, keepdims=True) + jnp.sum(p_n, axis=1, keepdims=True)
    o = (_dot_nt(p_c.astype(BF16), vtb) + _dot(p_n.astype(BF16), vn_ref[0])) / l
    b_out = jnp.zeros((T, D_B), F32)
    for hd in range(N_HEADS):
        b_out = b_out + jnp.where(head_of_lane == hd, o[hd * T:(hd + 1) * T], 0.0)
    o_ref[0] = _rms(b_out, og_ref[...]).astype(BF16)


def _attn_sample(q, kn, vn, lf, cache_kt, cache_vt, cache_lft, og, layer):
    G, T, _ = q.shape
    P = cache_kt.shape[3]
    row_spec = pl.BlockSpec((1, T, D_B), lambda b: (b, 0, 0))
    return pl.pallas_call(
        _attn_sample_kernel,
        grid=(G,),
        in_specs=[row_spec, row_spec, row_spec,
                  pl.BlockSpec((1, T, N_HEADS), lambda b: (b, 0, 0)),
                  pl.BlockSpec((1, 1, D_B, P), lambda b: (layer, b, 0, 0)),
                  pl.BlockSpec((1, 1, D_B, P), lambda b: (layer, b, 0, 0)),
                  pl.BlockSpec((None, 1, N_HEADS, P), lambda b: (layer, b, 0, 0)),
                  _full_spec((1, D_B))],
        out_specs=pl.BlockSpec((1, T, D_B), lambda b: (b, 0, 0)),
        out_shape=jax.ShapeDtypeStruct((G, T, D_B), BF16),
        compiler_params=pltpu.CompilerParams(dimension_semantics=("arbitrary",), vmem_limit_bytes=VMEM_LIMIT),
        name="attn_sample",
    )(q, kn, vn, lf, cache_kt, cache_vt, cache_lft, og)


def _k3_rows(x, ma, mb, mc, G, T, prm, hist, wb, ybuf):
    M = G * T
    wo = prm["wo"]
    x1 = x + _dot(ma, wo[0:D_A, :]) + _dot(mb, wo[D_A:D_A + D_B, :]) + _dot(mc, wo[D_A + D_B:D_MODEL, :])
    h2 = _rms(x1, prm["g2"][...]).astype(BF16)
    wup = prm["wup"]
    wcf = prm["wcf"]
    n_chunks = D_FF // FF_CHUNK

    def up_pair(c):
        return [_dot(h2, wup[:, kk * D_FF + c * FF_CHUNK:kk * D_FF + (c + 1) * FF_CHUNK]) for kk in range(2)]

    ups_next = up_pair(0)
    for c in range(n_chunks):
        ups = ups_next
        if c + 1 < n_chunks:
            ups_next = up_pair(c + 1)
        halves = []
        for kk in range(2):
            off = kk * D_FF + c * FF_CHUNK
            sl = slice(off, off + FF_CHUNK)
            up3 = ups[kk].reshape(G, T, FF_CHUNK)
            wb[kk, :, 0:HIST_A, :] = hist[:, :, sl]
            wb[kk, :, HIST_A:HIST_A + T, :] = up3
            conv = wcf[W_F - 1:W_F, sl].reshape(1, 1, FF_CHUNK) * up3
            for j in range(W_F - 1):
                o = HIST_A - (W_F - 1) + j
                conv = conv + wcf[j:j + 1, sl].reshape(1, 1, FF_CHUNK) * wb[kk, :, o:o + T, :]
            hist[:, :, sl] = wb[kk, :, T:T + HIST_A, :]
            halves.append(conv.reshape(M, FF_CHUNK))
        gate, val = halves
        ybuf[0:M, c * FF_CHUNK:(c + 1) * FF_CHUNK] = (gate * _sigmoid(gate) * val).astype(BF16)
    return x1 + _dot(ybuf[0:M, :], prm["wdn"][...])


_K3_PARAMS = ("wo", "g2", "wup", "wcf", "wdn")


def _k3_prompt_kernel(x_ref, ma_ref, mb_ref, mc_ref, wo, g2, wup, wcf, wdn, xo_ref, fo_ref, hist, wb, ybuf):
    prm = dict(zip(_K3_PARAMS, (wo, g2, wup, wcf, wdn)))
    j = pl.program_id(1)
    n_full = pl.num_programs(1) - 1

    @pl.when(j == 0)
    def _():
        hist[...] = jnp.zeros(hist.shape, F32)

    @pl.when(j < n_full)
    def _():
        xo_ref[0] = _k3_rows(x_ref[0], ma_ref[0], mb_ref[0], mc_ref[0], 1, ROW_TILE, prm, hist, wb, ybuf)

    @pl.when(j == n_full)
    def _():
        T = TAIL
        xo_ref[0, 0:T, :] = _k3_rows(x_ref[0, 0:T, :], ma_ref[0, 0:T, :], mb_ref[0, 0:T, :], mc_ref[0, 0:T, :],
                                     1, T, prm, hist, wb, ybuf)
        fo_ref[0] = hist[0, HIST_A - (W_F - 1):HIST_A, :]


def _k3_sample_kernel(x_ref, ma_ref, mb_ref, mc_ref, sf_ref, wo, g2, wup, wcf, wdn, xo_ref, fo_ref, hist, wb, ybuf):
    prm = dict(zip(_K3_PARAMS, (wo, g2, wup, wcf, wdn)))
    G = sf_ref.shape[0]
    hist[...] = jnp.zeros(hist.shape, F32)
    hist[:, HIST_A - (W_F - 1):HIST_A, :] = sf_ref[...]
    xo_ref[...] = _k3_rows(x_ref[...], ma_ref[...], mb_ref[...], mc_ref[...], G, TAIL, prm, hist, wb, ybuf)
    fo_ref[...] = hist[:, HIST_A - (W_F - 1):HIST_A, :]


def _k3_param_specs(layer):
    return [pl.BlockSpec((None, D_MODEL, D_MODEL), lambda *_: (layer, 0, 0), pipeline_mode=pl.Buffered(1)),
            _full_spec((1, D_MODEL)),
            pl.BlockSpec((None, D_MODEL, 2 * D_FF), lambda *_: (layer, 0, 0), pipeline_mode=pl.Buffered(1)),
            _full_spec((W_F, 2 * D_FF)),
            pl.BlockSpec((None, D_FF, D_MODEL), lambda *_: (layer, 0, 0), pipeline_mode=pl.Buffered(1))]


def _k3_prompt(x, ma, mb, mc, params, layer):
    B, L, _ = x.shape
    nt = pl.cdiv(L, ROW_TILE)
    row = lambda c: pl.BlockSpec((1, ROW_TILE, c), lambda b, j: (b, j, 0))
    return pl.pallas_call(
        _k3_prompt_kernel,
        grid=(B, nt),
        in_specs=[row(D_MODEL), row(D_A), row(D_B), row(D_C)] + _k3_param_specs(layer),
        out_specs=[row(D_MODEL), pl.BlockSpec((1, W_F - 1, 2 * D_FF), lambda b, j: (b, 0, 0))],
        out_shape=[jax.ShapeDtypeStruct((B, L, D_MODEL), F32), jax.ShapeDtypeStruct((B, W_F - 1, 2 * D_FF), F32)],
        scratch_shapes=[pltpu.VMEM((1, HIST_A, 2 * D_FF), F32),
                        pltpu.VMEM((2, 1, HIST_A + ROW_TILE, FF_CHUNK), F32),
                        pltpu.VMEM((ROW_TILE, D_FF), BF16)],
        compiler_params=pltpu.CompilerParams(dimension_semantics=("arbitrary", "arbitrary"),
                                             vmem_limit_bytes=VMEM_LIMIT),
        name="k3_prompt",
    )(x, ma, mb, mc, *params)


def _k3_sample(x, ma, mb, mc, state_f, params, layer):
    G, T, _ = x.shape
    M = G * T
    return pl.pallas_call(
        _k3_sample_kernel,
        grid=(1,),
        in_specs=[_full_spec((M, D_MODEL)), _full_spec((M, D_A)), _full_spec((M, D_B)), _full_spec((M, D_C)),
                  _full_spec((G, W_F - 1, 2 * D_FF))] + _k3_param_specs(layer),
        out_specs=[_full_spec((M, D_MODEL)), _full_spec((G, W_F - 1, 2 * D_FF))],
        out_shape=[jax.ShapeDtypeStruct((M, D_MODEL), F32), jax.ShapeDtypeStruct((G, W_F - 1, 2 * D_FF), F32)],
        scratch_shapes=[pltpu.VMEM((G, HIST_A, 2 * D_FF), F32),
                        pltpu.VMEM((2, G, HIST_A + T, FF_CHUNK), F32),
                        pltpu.VMEM((M, D_FF), BF16)],
        compiler_params=pltpu.CompilerParams(dimension_semantics=("arbitrary",), vmem_limit_bytes=VMEM_LIMIT),
        name="k3_sample",
    )(x.reshape(M, D_MODEL), ma, mb.reshape(M, D_B), mc, state_f, *params)


def kernel(x_prompt, x_sample, cache_k, cache_v, cache_logf, state_sconv, state_cconv, state_ffn, meta_tokens,
           norm1_g, w_in, b_f, q_norm_g, k_norm_g, w_conv_a, w_conv_c, b_conv_c, ln_c_g, ln_c_b, out_g_a, out_g_b,
           out_g_c, w_out, norm2_g, w_up, w_conv_f, w_down):
    depth = w_in.shape[0]
    bp, seq, _ = x_prompt.shape
    G, T, _ = x_sample.shape
    P = cache_k.shape[2]

    o3 = 3 * D_A
    o6 = o3 + 3 * D_B
    w_fl = jnp.pad(w_in[:, :, o6:o6 + N_HEADS], ((0, 0), (0, 0), (0, LANES - N_HEADS)))
    w_rows = jnp.concatenate([w_in[:, :, 0:o3], w_in[:, :, o6 + N_HEADS:], w_fl], axis=-1).astype(BF16)
    w_qkv = w_in[:, :, o3:o6].astype(BF16)
    w_qkv_t = jnp.swapaxes(w_qkv, 1, 2)
    w_out_b = w_out.astype(BF16)
    w_up_b = w_up.astype(BF16)
    w_dn_b = w_down.astype(BF16)
    bf_pad = jnp.pad(b_f, ((0, 0), (0, LANES - N_HEADS)))
    q_scale = (HEAD_DIM ** -0.5) * LOG2E
    cache_kt = jnp.transpose(cache_k, (0, 1, 3, 4, 2)).reshape(depth, G, D_B, P)
    cache_vt = jnp.transpose(cache_v, (0, 1, 3, 4, 2)).reshape(depth, G, D_B, P)
    cache_lft = jnp.swapaxes(cache_logf, 2, 3)

    def k1_params(l, cols):
        r = lambda a: a[l][None, :]
        if cols:
            qg = jnp.broadcast_to((q_norm_g[l] * q_scale)[:, None], (HEAD_DIM, ROW_TILE))
            kg = jnp.broadcast_to(k_norm_g[l][:, None], (HEAD_DIM, ROW_TILE))
            wqkv = w_qkv_t
        else:
            qg = jnp.tile(q_norm_g[l] * q_scale, N_HEADS)[None, :]
            kg = jnp.tile(k_norm_g[l], N_HEADS)[None, :]
            wqkv = w_qkv
        return (r(norm1_g), w_rows, wqkv, r(bf_pad), qg, kg, w_conv_a[l], w_conv_c[l], r(b_conv_c),
                r(ln_c_g), r(ln_c_b), r(out_g_a), r(out_g_c))

    def k3_params(l):
        return (w_out_b, norm2_g[l][None, :], w_up_b, w_conv_f[l], w_dn_b)

    meta = jnp.broadcast_to(meta_tokens[None], (bp, N_META, D_MODEL))
    xp = jnp.concatenate([meta, x_prompt], axis=1)
    L = N_META + seq
    xs = x_sample.reshape(G * T, D_MODEL)

    kp_, vp_, fp_, ap_, cp_, ffp_ = [], [], [], [], [], []
    ks_, vs_, fs_, as_, cs_, ffs_ = [], [], [], [], [], []
    for l in range(depth):
        ma, mc, qt, kpair, vt, kf, vf, lft, ct, sa, sc = _k1_prompt(xp, k1_params(l, True), l)
        mb = _attn_prompt(qt, kpair, vt, ct, out_g_b[l][None, :], L)
        xp, sf = _k3_prompt(xp, ma, mb, mc, k3_params(l), l)
        to_out = lambda a: jnp.transpose(a.reshape(bp, N_HEADS, HEAD_DIM, L), (0, 3, 1, 2))
        kp_.append(to_out(kf)); vp_.append(to_out(vf)); fp_.append(jnp.swapaxes(lft, 1, 2))
        ap_.append(sa); cp_.append(sc); ffp_.append(sf)

        ma, mc, q, k, v, kf, vf, lf, sa, sc = _k1_sample(xs.reshape(G, T, D_MODEL), state_sconv[l],
                                                         state_cconv[l], k1_params(l, False), l)
        r3 = lambda a: a.reshape(G, T, D_B)
        mb = _attn_sample(r3(q), r3(k), r3(v), lf.reshape(G, T, N_HEADS), cache_kt, cache_vt, cache_lft,
                          out_g_b[l][None, :], l)
        xs, sf = _k3_sample(xs.reshape(G, T, D_MODEL), ma, mb, mc, state_ffn[l], k3_params(l), l)
        ks_.append(kf.reshape(G, T, N_HEADS, HEAD_DIM)); vs_.append(vf.reshape(G, T, N_HEADS, HEAD_DIM))
        fs_.append(lf.reshape(G, T, N_HEADS)); as_.append(sa); cs_.append(sc); ffs_.append(sf)

    y_prompt = xp[:, N_META:]
    y_sample = xs.reshape(G, T, D_MODEL)
    st = jnp.stack
    return (y_prompt, y_sample, st(kp_), st(vp_), st(fp_), st(ap_), st(cp_), st(ffp_),
            st(ks_), st(vs_), st(fs_), st(as_), st(cs_), st(ffs_))
```
